```python
import math
import jax, jax.numpy as jnp
from jax import lax
import numpy as np

D_MODEL = 1024
BATCH = 4
SEQ = 8192
DEPTH = 1

CHUNK = 64
N_ATT_HEADS = 4
HEAD_DIM = 64
V_DIM = 2 * HEAD_DIM
ATT_WIDTH = N_ATT_HEADS * V_DIM
QK_WIDTH = N_ATT_HEADS * 2 * HEAD_DIM
N_SGU_GROUPS = 4
SGU_GROUP_DIM = 128
SGU_WIDTH = N_SGU_GROUPS * SGU_GROUP_DIM
SGU_CHUNK = 128
N_BRANCHES = 2
D_FF = 4 * D_MODEL
N_BUCKETS = 32
MAX_DISTANCE = 128
Q_BLOCK = 128
NORM_EPS = 1e-6
IN_WIDTH = 2 * QK_WIDTH + ATT_WIDTH + 2 * SGU_WIDTH + N_BRANCHES * D_MODEL
SPLITS = (QK_WIDTH, 2 * QK_WIDTH, 2 * QK_WIDTH + ATT_WIDTH, 2 * QK_WIDTH + ATT_WIDTH + 2 * SGU_WIDTH)

kernel_name = 'hybrid_diffattn_sgu_gated_block'


def rmsnorm(x, g):
    xf = x.astype(jnp.float32)
    y = xf * lax.rsqrt(jnp.mean(xf * xf, axis=-1, keepdims=True) + NORM_EPS)
    return (y * g.astype(jnp.float32)).astype(x.dtype)


def t5_bucket(rel):
    half = N_BUCKETS // 2
    ret = (rel > 0).astype(jnp.int32) * half
    n = jnp.abs(rel)
    max_exact = half // 2
    nf = jnp.maximum(n, 1).astype(jnp.float32)
    large = max_exact + (jnp.log(nf / max_exact) / math.log(MAX_DISTANCE / max_exact)
                         * (half - max_exact)).astype(jnp.int32)
    large = jnp.minimum(large, half - 1)
    return ret + jnp.where(n < max_exact, n, large)


def rel_bias_by_distance(rel_bias, seq):
    d = jnp.arange(-(seq - 1), seq, dtype=jnp.int32)
    return rel_bias[t5_bucket(d)]


def diff_attention(q1, q2, k1, k2, v, lam, bias_dist):
    b, h, s, _ = q1.shape
    nb = s // Q_BLOCK

    def to_blocks(t):
        return t.reshape(b, h, nb, Q_BLOCK, t.shape[-1]).transpose(2, 0, 1, 3, 4)

    k_pos = jnp.arange(s, dtype=jnp.int32)
    scale = HEAD_DIM ** -0.5

    def block(args):
        i, q1b, q2b = args
        q_pos = i * Q_BLOCK + jnp.arange(Q_BLOCK, dtype=jnp.int32)
        mask = (k_pos // CHUNK)[None, :] <= (q_pos // CHUNK)[:, None]
        bias = bias_dist[k_pos[None, :] - q_pos[:, None] + (s - 1)]
        bias = jnp.transpose(bias, (2, 0, 1)).astype(jnp.float32)[None]

        def probs(qb, kk):
            logits = jnp.einsum('bhqd,bhkd->bhqk', qb, kk).astype(jnp.float32) * scale + bias
            return jax.nn.softmax(jnp.where(mask, logits, -jnp.inf), axis=-1)

        weights = probs(q1b, k1) - lam * probs(q2b, k2)
        return jnp.einsum('bhqk,bhkd->bhqd', weights.astype(v.dtype), v)

    out = lax.map(block, (jnp.arange(nb, dtype=jnp.int32), to_blocks(q1), to_blocks(q2)))
    return out.transpose(1, 0, 3, 2, 4).reshape(b, s, h, v.shape[-1])


def spatial_gating(uv, norm_g, w_s, b_s):
    b, s, _ = uv.shape
    u, v = jnp.split(uv, 2, axis=-1)
    v = rmsnorm(v.reshape(b, s, N_SGU_GROUPS, SGU_GROUP_DIM), norm_g)
    causal = jnp.tril(jnp.ones((SGU_CHUNK, SGU_CHUNK), dtype=bool))
    w = jnp.where(causal[None], w_s, 0).astype(v.dtype)
    vc = v.reshape(b, s // SGU_CHUNK, SGU_CHUNK, N_SGU_GROUPS, SGU_GROUP_DIM)
    mixed = jnp.einsum('gts,bnsgc->bntgc', w, vc) + b_s.T[None, None, :, :, None]
    return u * mixed.reshape(b, s, SGU_WIDTH)


def hybrid_layer(x, layer_idx, norm1_g, w_in, lam_q1, lam_k1, lam_q2, lam_k2, subln_g,
                 bias_dist, sgu_norm_g, w_spatial, b_spatial, w_proj_attn, w_proj_sgu,
                 w_out, norm2_g, w_ff1, w_ff2):
    b, s, _ = x.shape
    lam_init = 0.8 - 0.6 * math.exp(-0.3 * layer_idx)
    xn = rmsnorm(x, norm1_g)
    z = jnp.einsum('bsd,de->bse', xn, w_in)
    zq, zk, zv, zuv, zg = jnp.split(z, SPLITS, axis=-1)

    q = zq.reshape(b, s, N_ATT_HEADS, 2, HEAD_DIM).transpose(3, 0, 2, 1, 4)
    k = zk.reshape(b, s, N_ATT_HEADS, 2, HEAD_DIM).transpose(3, 0, 2, 1, 4)
    v = zv.reshape(b, s, N_ATT_HEADS, V_DIM).transpose(0, 2, 1, 3)
    f32 = jnp.float32
    lam = (jnp.exp(jnp.sum(lam_q1.astype(f32) * lam_k1.astype(f32)))
           - jnp.exp(jnp.sum(lam_q2.astype(f32) * lam_k2.astype(f32))) + lam_init)
    attn = diff_attention(q[0], q[1], k[0], k[1], v, lam, bias_dist)
    attn = rmsnorm(attn, subln_g) * (1.0 - lam_init)
    y_attn = jnp.einsum('bse,ed->bsd', attn.reshape(b, s, ATT_WIDTH), w_proj_attn)

    sgu = spatial_gating(jax.nn.gelu(zuv, approximate=False), sgu_norm_g, w_spatial, b_spatial)
    y_sgu = jnp.einsum('bse,ed->bsd', sgu, w_proj_sgu)

    g_attn, g_sgu = jnp.split(jax.nn.sigmoid(zg), 2, axis=-1)
    merged = g_attn * y_attn + g_sgu * y_sgu
    h = x + jnp.einsum('bsd,de->bse', merged, w_out)

    hn = rmsnorm(h, norm2_g)
    ff = jnp.square(jax.nn.relu(jnp.einsum('bsd,df->bsf', hn, w_ff1)))
    return h + jnp.einsum('bsf,fd->bsd', ff, w_ff2)


def setup_inputs(seed: int = 0) -> dict:
    key = jax.random.key(seed)
    ks = jax.random.split(key, 20)

    def nrm(k, shape, scale):
        return jax.random.normal(k, shape, jnp.float32) * scale

    def gain(k, shape):
        return 1.0 + 0.02 * jax.random.normal(k, shape, jnp.float32)

    return {
        'x': nrm(ks[0], (BATCH, SEQ, D_MODEL), 1.0),
        'norm1_g': gain(ks[1], (DEPTH, D_MODEL)),
        'w_in': nrm(ks[2], (DEPTH, D_MODEL, IN_WIDTH), D_MODEL ** -0.5),
        'lam_q1': nrm(ks[3], (DEPTH, HEAD_DIM), 0.1),
        'lam_k1': nrm(ks[4], (DEPTH, HEAD_DIM), 0.1),
        'lam_q2': nrm(ks[5], (DEPTH, HEAD_DIM), 0.1),
        'lam_k2': nrm(ks[6], (DEPTH, HEAD_DIM), 0.1),
        'subln_g': gain(ks[7], (DEPTH, V_DIM)),
        'rel_bias': nrm(ks[8], (N_BUCKETS, N_ATT_HEADS), 0.5),
        'sgu_norm_g': gain(ks[9], (DEPTH, N_SGU_GROUPS, SGU_GROUP_DIM)),
        'w_spatial': nrm(ks[10], (DEPTH, N_SGU_GROUPS, SGU_CHUNK, SGU_CHUNK), SGU_CHUNK ** -0.5),
        'b_spatial': 1.0 + nrm(ks[11], (DEPTH, N_SGU_GROUPS, SGU_CHUNK), 0.02),
        'w_proj_attn': nrm(ks[12], (DEPTH, ATT_WIDTH, D_MODEL), ATT_WIDTH ** -0.5),
        'w_proj_sgu': nrm(ks[13], (DEPTH, SGU_WIDTH, D_MODEL), SGU_WIDTH ** -0.5),
        'w_out': nrm(ks[14], (DEPTH, D_MODEL, D_MODEL), D_MODEL ** -0.5),
        'norm2_g': gain(ks[15], (DEPTH, D_MODEL)),
        'w_ff1': nrm(ks[16], (DEPTH, D_MODEL, D_FF), D_MODEL ** -0.5),
        'w_ff2': nrm(ks[17], (DEPTH, D_FF, D_MODEL), D_FF ** -0.5),
        'normf_g': gain(ks[18], (D_MODEL,)),
    }


def reference(x, norm1_g, w_in, lam_q1, lam_k1, lam_q2, lam_k2, subln_g, rel_bias,
              sgu_norm_g, w_spatial, b_spatial, w_proj_attn, w_proj_sgu, w_out,
              norm2_g, w_ff1, w_ff2, normf_g):
    bias_dist = rel_bias_by_distance(rel_bias, x.shape[1])
    h = x
    for l in range(DEPTH):
        h = hybrid_layer(h, l, norm1_g[l], w_in[l], lam_q1[l], lam_k1[l], lam_q2[l], lam_k2[l],
                         subln_g[l], bias_dist, sgu_norm_g[l], w_spatial[l], b_spatial[l],
                         w_proj_attn[l], w_proj_sgu[l], w_out[l], norm2_g[l], w_ff1[l], w_ff2[l])
    return rmsnorm(h, normf_g)
```

```python
import functools
import math

import jax
import jax.numpy as jnp
from jax import lax
from jax.experimental import pallas as pl
from jax.experimental.pallas import tpu as pltpu

F32 = jnp.float32
BF16 = jnp.bfloat16

CHUNK = 64
N_ATT_HEADS = 4
HEAD_DIM = 64
V_DIM = 2 * HEAD_DIM
QK_WIDTH = N_ATT_HEADS * 2 * HEAD_DIM
ATT_WIDTH = N_ATT_HEADS * V_DIM
N_SGU_GROUPS = 4
SGU_GROUP_DIM = 128
SGU_WIDTH = N_SGU_GROUPS * SGU_GROUP_DIM
SGU_CHUNK = 128
N_BUCKETS = 32
MAX_DISTANCE = 128
NORM_EPS = 1e-6
ATTN_SCALE = HEAD_DIM ** -0.5

V7X_VMEM_BYTES = 64 * 1024 * 1024
V7X_LANES = 128

TOKEN_TILE = 512
ATTN_TILE = 256
FF_CHUNK = 1024

_NT_DIMS = (((1,), (1,)), ((), ()))


def _vmem_limit(block_bytes, temp_bytes):
    return int(min(2 * block_bytes + temp_bytes, V7X_VMEM_BYTES))


def _nbytes(shape, dtype):
    return math.prod(shape) * jnp.dtype(dtype).itemsize


def _rms_scale(x):
    return lax.rsqrt(jnp.mean(x * x, axis=-1, keepdims=True) + NORM_EPS)


def _in_proj_kernel(x_ref, g1_ref, wk_ref, wqt_ref, wvt_ref, wuv_ref, sgn_ref, ws_ref, bs_ref,
                    k_ref, qt_ref, vt_ref, sgu_ref):
    x = x_ref[...]
    xn = (x * _rms_scale(x) * g1_ref[...]).astype(BF16)
    k_ref[...] = jnp.dot(xn, wk_ref[...], preferred_element_type=F32).astype(BF16)
    qt = lax.dot_general(wqt_ref[...], xn, _NT_DIMS, preferred_element_type=F32)
    qt_ref[...] = (qt * ATTN_SCALE).astype(BF16)
    vt_ref[...] = lax.dot_general(wvt_ref[...], xn, _NT_DIMS, preferred_element_type=F32).astype(BF16)

    zuv = jnp.dot(xn, wuv_ref[...], preferred_element_type=F32)
    guv = 0.5 * zuv * (1.0 + lax.erf(zuv * (2.0 ** -0.5)))
    tm = x.shape[0]
    t_idx = lax.broadcasted_iota(jnp.int32, (SGU_CHUNK, SGU_CHUNK), 0)
    s_idx = lax.broadcasted_iota(jnp.int32, (SGU_CHUNK, SGU_CHUNK), 1)
    for g in range(N_SGU_GROUPS):
        lo = g * SGU_GROUP_DIM
        u = guv[:, lo:lo + SGU_GROUP_DIM]
        v = guv[:, SGU_WIDTH + lo:SGU_WIDTH + lo + SGU_GROUP_DIM]
        vn = (v * _rms_scale(v) * sgn_ref[g:g + 1, :]).astype(BF16)
        w = jnp.where(s_idx <= t_idx, ws_ref[g], 0.0).astype(BF16)
        b = bs_ref[:, lo:lo + SGU_GROUP_DIM]
        for c in range(tm // SGU_CHUNK):
            r0 = c * SGU_CHUNK
            mixed = jnp.dot(w, vn[r0:r0 + SGU_CHUNK, :], preferred_element_type=F32) + b
            sgu_ref[r0:r0 + SGU_CHUNK, lo:lo + SGU_GROUP_DIM] = (u[r0:r0 + SGU_CHUNK, :] * mixed).astype(BF16)


def _in_proj(x2, g1, wk, wqt, wvt, wuv, sgn, ws, bs_tile):
    n, d = x2.shape
    tm = TOKEN_TILE
    full = lambda a: pl.BlockSpec(a.shape, lambda i: (0,) * a.ndim)
    in_specs = [pl.BlockSpec((tm, d), lambda i: (i, 0)),
                full(g1), full(wk), full(wqt), full(wvt), full(wuv), full(sgn), full(ws), full(bs_tile)]
    out_shape = (jax.ShapeDtypeStruct((n, QK_WIDTH), BF16),
                 jax.ShapeDtypeStruct((QK_WIDTH, n), BF16),
                 jax.ShapeDtypeStruct((ATT_WIDTH, n), BF16),
                 jax.ShapeDtypeStruct((n, SGU_WIDTH), BF16))
    out_specs = (pl.BlockSpec((tm, QK_WIDTH), lambda i: (i, 0)),
                 pl.BlockSpec((QK_WIDTH, tm), lambda i: (0, i)),
                 pl.BlockSpec((ATT_WIDTH, tm), lambda i: (0, i)),
                 pl.BlockSpec((tm, SGU_WIDTH), lambda i: (i, 0)))
    blocks = (_nbytes((tm, d), F32) + sum(_nbytes(a.shape, a.dtype) for a in (g1, wk, wqt, wvt, wuv, sgn, ws, bs_tile))
              + 4 * _nbytes((tm, QK_WIDTH), BF16))
    temps = _nbytes((tm, d), F32) + 6 * _nbytes((tm, 2 * SGU_WIDTH), F32)
    return pl.pallas_call(
        _in_proj_kernel, grid=(n // tm,), in_specs=in_specs, out_specs=out_specs, out_shape=out_shape,
        compiler_params=pltpu.CompilerParams(dimension_semantics=("arbitrary",),
                                             vmem_limit_bytes=_vmem_limit(blocks, temps)),
        name="in_proj",
    )(x2, g1, wk, wqt, wvt, wuv, sgn, ws, bs_tile)


def _attn_kernel(lam_ref, qt_ref, k_ref, vt_ref, bias_ref, g_ref, o_ref, acc1_ref, acc2_ref, *, lam_init):
    t = qt_ref.shape[1]
    i = pl.program_id(2)
    qt = qt_ref[...]
    row = lax.broadcasted_iota(jnp.int32, qt.shape, 0)
    zero = jnp.zeros_like(qt)
    q1 = jnp.where(row < HEAD_DIM, qt, zero)
    q2 = jnp.where(row >= HEAD_DIM, qt, zero)
    acc1_ref[...] = jnp.zeros_like(acc1_ref)
    acc2_ref[...] = jnp.zeros_like(acc2_ref)

    def block(start, bias, carry):
        kb = k_ref[pl.ds(start, t), :]
        vb = vt_ref[:, pl.ds(start, t)]
        new = []
        for qp, acc_ref, (m, l) in ((q1, acc1_ref, carry[0]), (q2, acc2_ref, carry[1])):
            s = jnp.dot(kb, qp, preferred_element_type=F32)
            if bias is not None:
                s = s + bias
            m_new = jnp.maximum(m, jnp.max(s, axis=0, keepdims=True))
            alpha = jnp.exp(m - m_new)
            p = jnp.exp(s - m_new)
            l_new = alpha * l + jnp.sum(p, axis=0, keepdims=True)
            acc_ref[...] = alpha * acc_ref[...] + jnp.dot(vb, p.astype(BF16), preferred_element_type=F32)
            new.append((m_new, l_new))
        return tuple(new)

    init = (jnp.full((1, t), -jnp.inf, F32), jnp.zeros((1, t), F32))
    carry = (init, init)
    carry = lax.fori_loop(0, jnp.maximum(i - 1, 0),
                          lambda j, c: block(pl.multiple_of(j * t, t), None, c), carry)

    def prev_block(c):
        return block(pl.multiple_of((i - 1) * t, t), bias_ref[0], c)

    carry = lax.cond(i > 0, prev_block, lambda c: c, carry)
    (m1, l1), (m2, l2) = block(pl.multiple_of(i * t, t), bias_ref[1], carry)

    lp = lam_ref[...]
    lam = (jnp.exp(jnp.sum(lp[0:1] * lp[1:2], axis=-1, keepdims=True))
           - jnp.exp(jnp.sum(lp[2:3] * lp[3:4], axis=-1, keepdims=True)) + lam_init)
    o = acc1_ref[...] * (1.0 / l1) - lam * (acc2_ref[...] * (1.0 / l2))
    o = o * lax.rsqrt(jnp.mean(o * o, axis=0, keepdims=True) + NORM_EPS)
    o_ref[...] = ((o.T * g_ref[...]) * (1.0 - lam_init)).astype(BF16)


def _t5_bucket(rel):
    half = N_BUCKETS // 2
    ret = (rel > 0).astype(jnp.int32) * half
    n = jnp.abs(rel)
    max_exact = half // 2
    nf = jnp.maximum(n, 1).astype(F32)
    large = max_exact + (jnp.log(nf / max_exact) / math.log(MAX_DISTANCE / max_exact)
                         * (half - max_exact)).astype(jnp.int32)
    large = jnp.minimum(large, half - 1)
    return ret + jnp.where(n < max_exact, n, large)


def _bias_tiles(rel_bias, seq, t):
    kk = jnp.arange(t, dtype=jnp.int32)[:, None]
    qq = jnp.arange(t, dtype=jnp.int32)[None, :]
    rel = jnp.stack([kk - qq - t, kk - qq])
    bias = jnp.transpose(rel_bias[_t5_bucket(rel)], (3, 0, 1, 2)).astype(F32)
    far = rel_bias[_t5_bucket(jnp.asarray(-(seq - 1), jnp.int32))].astype(F32)
    bias = bias - far[:, None, None, None]
    visible = jnp.stack([jnp.ones((t, t), bool), (kk // CHUNK) <= (qq // CHUNK)])
    return jnp.where(visible[None], bias, -jnp.inf)


def _diff_attn(qt, k, vt, bias, lam_rows, subln_g, batch, seq, lam_init):
    t = ATTN_TILE
    assert t >= MAX_DISTANCE and t % CHUNK == 0 and seq % t == 0
    n = batch * seq
    nq = seq // t
    in_specs = [pl.BlockSpec(lam_rows.shape, lambda b, h, i: (0, 0)),
                pl.BlockSpec((2 * HEAD_DIM, t), lambda b, h, i: (h, b * nq + i)),
                pl.BlockSpec((seq, 2 * HEAD_DIM), lambda b, h, i: (b, h)),
                pl.BlockSpec((V_DIM, seq), lambda b, h, i: (h, b)),
                pl.BlockSpec((None, 2, t, t), lambda b, h, i: (h, 0, 0, 0)),
                pl.BlockSpec(subln_g.shape, lambda b, h, i: (0, 0))]
    out_specs = pl.BlockSpec((t, V_DIM), lambda b, h, i: (b * nq + i, h))
    blocks = (_nbytes((2 * HEAD_DIM, t), BF16) + 2 * _nbytes((seq, 2 * HEAD_DIM), BF16)
              + _nbytes((2, t, t), F32) + _nbytes((t, V_DIM), BF16))
    temps = 2 * _nbytes((V_DIM, t), F32) + 8 * _nbytes((t, t), F32)
    return pl.pallas_call(
        functools.partial(_attn_kernel, lam_init=lam_init),
        grid=(batch, N_ATT_HEADS, nq), in_specs=in_specs, out_specs=out_specs,
        out_shape=jax.ShapeDtypeStruct((n, ATT_WIDTH), BF16),
        scratch_shapes=[pltpu.VMEM((V_DIM, t), F32), pltpu.VMEM((V_DIM, t), F32)],
        compiler_params=pltpu.CompilerParams(dimension_semantics=("arbitrary",) * 3,
                                             vmem_limit_bytes=_vmem_limit(blocks, temps)),
        name="diff_attn",
    )(lam_rows, qt, k, vt, bias, subln_g)


def _merge_kernel(x_ref, g1_ref, attn_ref, sgu_ref, wg_ref, wpa_ref, wps_ref, wo_ref, h_ref):
    x = x_ref[...]
    d = x.shape[1]
    xn = (x * _rms_scale(x) * g1_ref[...]).astype(BF16)
    gate = jax.nn.sigmoid(jnp.dot(xn, wg_ref[...], preferred_element_type=F32))
    y_attn = jnp.dot(attn_ref[...], wpa_ref[...], preferred_element_type=F32)
    y_sgu = jnp.dot(sgu_ref[...], wps_ref[...], preferred_element_type=F32)
    merged = gate[:, :d] * y_attn + gate[:, d:] * y_sgu
    h_ref[...] = x + jnp.dot(merged.astype(BF16), wo_ref[...], preferred_element_type=F32)


def _merge(x2, g1, attn, sgu, wg, wpa, wps, wo):
    n, d = x2.shape
    tm = TOKEN_TILE
    full = lambda a: pl.BlockSpec(a.shape, lambda i: (0,) * a.ndim)
    row = lambda w: pl.BlockSpec((tm, w), lambda i: (i, 0))
    blocks = (2 * _nbytes((tm, d), F32) + 2 * _nbytes((tm, ATT_WIDTH), BF16)
              + sum(_nbytes(a.shape, a.dtype) for a in (g1, wg, wpa, wps, wo)))
    temps = 8 * _nbytes((tm, d), F32)
    return pl.pallas_call(
        _merge_kernel, grid=(n // tm,),
        in_specs=[row(d), full(g1), row(ATT_WIDTH), row(SGU_WIDTH), full(wg), full(wpa), full(wps), full(wo)],
        out_specs=row(d), out_shape=jax.ShapeDtypeStruct((n, d), F32),
        compiler_params=pltpu.CompilerParams(dimension_semantics=("arbitrary",),
                                             vmem_limit_bytes=_vmem_limit(blocks, temps)),
        name="merge",
    )(x2, g1, attn, sgu, wg, wpa, wps, wo)


def _ffn_kernel(h_ref, g2_ref, w1_ref, w2_ref, gf_ref, o_ref, *, final_norm):
    h = h_ref[...]
    hn = (h * _rms_scale(h) * g2_ref[...]).astype(BF16)
    d_ff = w1_ref.shape[1]
    out = h
    for c in range(d_ff // FF_CHUNK):
        lo = c * FF_CHUNK
        z = jnp.dot(hn, w1_ref[:, lo:lo + FF_CHUNK], preferred_element_type=F32)
        ff = jnp.square(jnp.maximum(z, 0.0)).astype(BF16)
        out = out + jnp.dot(ff, w2_ref[lo:lo + FF_CHUNK, :], preferred_element_type=F32)
    if final_norm:
        out = out * _rms_scale(out) * gf_ref[...]
    o_ref[...] = out


def _ffn(h, g2, w1, w2, gf, final_norm):
    n, d = h.shape
    tm = TOKEN_TILE
    full = lambda a: pl.BlockSpec(a.shape, lambda i: (0,) * a.ndim)
    row = pl.BlockSpec((tm, d), lambda i: (i, 0))
    blocks = 2 * _nbytes((tm, d), F32) + sum(_nbytes(a.shape, a.dtype) for a in (g2, w1, w2, gf))
    temps = 4 * _nbytes((tm, d), F32) + 3 * _nbytes((tm, FF_CHUNK), F32)
    return pl.pallas_call(
        functools.partial(_ffn_kernel, final_norm=final_norm), grid=(n // tm,),
        in_specs=[row, full(g2), full(w1), full(w2), full(gf)],
        out_specs=row, out_shape=jax.ShapeDtypeStruct((n, d), F32),
        compiler_params=pltpu.CompilerParams(dimension_semantics=("arbitrary",),
                                             vmem_limit_bytes=_vmem_limit(blocks, temps)),
        name="ffn",
    )(h, g2, w1, w2, gf)


def _layer(h, layer_idx, batch, seq, norm1_g, w_in, lam_rows, subln_g, bias, sgu_norm_g, w_spatial,
           b_spatial, w_proj_attn, w_proj_sgu, w_out, norm2_g, w_ff1, w_ff2, normf_g, final_norm):
    lam_init = 0.8 - 0.6 * math.exp(-0.3 * layer_idx)
    o_q, o_k, o_v, o_uv = QK_WIDTH, 2 * QK_WIDTH, 2 * QK_WIDTH + ATT_WIDTH, 2 * QK_WIDTH + ATT_WIDTH + 2 * SGU_WIDTH
    g1 = norm1_g[None, :]
    wqt = w_in[:, :o_q].T.astype(BF16)
    wk = w_in[:, o_q:o_k].astype(BF16)
    wvt = w_in[:, o_k:o_v].T.astype(BF16)
    wuv = w_in[:, o_v:o_uv].astype(BF16)
    wg = w_in[:, o_uv:].astype(BF16)
    bs_tile = jnp.repeat(b_spatial.T, SGU_GROUP_DIM, axis=1)
    k, qt, vt, sgu = _in_proj(h, g1, wk, wqt, wvt, wuv, sgu_norm_g, w_spatial, bs_tile)
    attn = _diff_attn(qt, k, vt, bias, lam_rows, subln_g[None, :], batch, seq, lam_init)
    h = _merge(h, g1, attn, sgu, wg, w_proj_attn.astype(BF16), w_proj_sgu.astype(BF16), w_out.astype(BF16))
    return _ffn(h, norm2_g[None, :], w_ff1.astype(BF16), w_ff2.astype(BF16), normf_g[None, :], final_norm)


def kernel(x, norm1_g, w_in, lam_q1, lam_k1, lam_q2, lam_k2, subln_g, rel_bias, sgu_norm_g, w_spatial,
           b_spatial, w_proj_attn, w_proj_sgu, w_out, norm2_g, w_ff1, w_ff2, normf_g):
    batch, seq, d = x.shape
    depth = w_in.shape[0]
    bias = _bias_tiles(rel_bias, seq, ATTN_TILE)
    h = x.reshape(batch * seq, d)
    for l in range(depth):
        lam_rows = jnp.stack([lam_q1[l], lam_k1[l], lam_q2[l], lam_k2[l]])
        h = _layer(h, l, batch, seq, norm1_g[l], w_in[l], lam_rows, subln_g[l], bias, sgu_norm_g[l],
                   w_spatial[l], b_spatial[l], w_proj_attn[l], w_proj_sgu[l], w_out[l], norm2_g[l],
                   w_ff1[l], w_ff2[l], normf_g, l == depth - 1)
    return h.reshape(batch, seq, d)
```

```python
import functools
import math

import jax
import jax.numpy as jnp
from jax import lax
from jax.experimental import pallas as pl
from jax.experimental.pallas import tpu as pltpu

F32 = jnp.float32
BF16 = jnp.bfloat16

CHUNK = 64
N_ATT_HEADS = 4
HEAD_DIM = 64
V_DIM = 2 * HEAD_DIM
QK_WIDTH = N_ATT_HEADS * 2 * HEAD_DIM
ATT_WIDTH = N_ATT_HEADS * V_DIM
N_SGU_GROUPS = 4
SGU_GROUP_DIM = 128
SGU_WIDTH = N_SGU_GROUPS * SGU_GROUP_DIM
SGU_CHUNK = 128
N_BUCKETS = 32
MAX_DISTANCE = 128
NORM_EPS = 1e-6
ATTN_SCALE = HEAD_DIM ** -0.5

V7X_VMEM_BYTES = 64 * 1024 * 1024
V7X_LANES = 128

TOKEN_TILE = 512
ATTN_TILE = 1024
NEAR_ROWS = 128
FF_CHUNK = 1024

_NT_DIMS = (((1,), (1,)), ((), ()))


def _vmem_limit(block_bytes, temp_bytes):
    return int(min(2 * block_bytes + temp_bytes, V7X_VMEM_BYTES))


def _nbytes(shape, dtype):
    return math.prod(shape) * jnp.dtype(dtype).itemsize


def _rms_scale(x):
    return lax.rsqrt(jnp.mean(x * x, axis=-1, keepdims=True) + NORM_EPS)


def _in_proj_kernel(x_ref, g1_ref, wk_ref, wqt_ref, wvt_ref, wuv_ref, sgn_ref, ws_ref, bs_ref,
                    k_ref, qt_ref, vt_ref, sgu_ref):
    x = x_ref[...]
    xn = (x * _rms_scale(x) * g1_ref[...]).astype(BF16)
    k_ref[...] = jnp.dot(xn, wk_ref[...], preferred_element_type=F32).astype(BF16)
    qt = lax.dot_general(wqt_ref[...], xn, _NT_DIMS, preferred_element_type=F32)
    qt_ref[...] = (qt * ATTN_SCALE).astype(BF16)
    vt_ref[...] = lax.dot_general(wvt_ref[...], xn, _NT_DIMS, preferred_element_type=F32).astype(BF16)

    zuv = jnp.dot(xn, wuv_ref[...], preferred_element_type=F32)
    guv = 0.5 * zuv * (1.0 + lax.erf(zuv * (2.0 ** -0.5)))
    tm = x.shape[0]
    t_idx = lax.broadcasted_iota(jnp.int32, (SGU_CHUNK, SGU_CHUNK), 0)
    s_idx = lax.broadcasted_iota(jnp.int32, (SGU_CHUNK, SGU_CHUNK), 1)
    for g in range(N_SGU_GROUPS):
        lo = g * SGU_GROUP_DIM
        u = guv[:, lo:lo + SGU_GROUP_DIM]
        v = guv[:, SGU_WIDTH + lo:SGU_WIDTH + lo + SGU_GROUP_DIM]
        vn = (v * _rms_scale(v) * sgn_ref[g:g + 1, :]).astype(BF16)
        w = jnp.where(s_idx <= t_idx, ws_ref[g], 0.0).astype(BF16)
        b = bs_ref[:, lo:lo + SGU_GROUP_DIM]
        for c in range(tm // SGU_CHUNK):
            r0 = c * SGU_CHUNK
            mixed = jnp.dot(w, vn[r0:r0 + SGU_CHUNK, :], preferred_element_type=F32) + b
            sgu_ref[r0:r0 + SGU_CHUNK, lo:lo + SGU_GROUP_DIM] = (u[r0:r0 + SGU_CHUNK, :] * mixed).astype(BF16)


def _in_proj(x2, g1, wk, wqt, wvt, wuv, sgn, ws, bs_tile):
    n, d = x2.shape
    tm = TOKEN_TILE
    full = lambda a: pl.BlockSpec(a.shape, lambda i: (0,) * a.ndim)
    in_specs = [pl.BlockSpec((tm, d), lambda i: (i, 0)),
                full(g1), full(wk), full(wqt), full(wvt), full(wuv), full(sgn), full(ws), full(bs_tile)]
    out_shape = (jax.ShapeDtypeStruct((n, QK_WIDTH), BF16),
                 jax.ShapeDtypeStruct((QK_WIDTH, n), BF16),
                 jax.ShapeDtypeStruct((ATT_WIDTH, n), BF16),
                 jax.ShapeDtypeStruct((n, SGU_WIDTH), BF16))
    out_specs = (pl.BlockSpec((tm, QK_WIDTH), lambda i: (i, 0)),
                 pl.BlockSpec((QK_WIDTH, tm), lambda i: (0, i)),
                 pl.BlockSpec((ATT_WIDTH, tm), lambda i: (0, i)),
                 pl.BlockSpec((tm, SGU_WIDTH), lambda i: (i, 0)))
    blocks = (_nbytes((tm, d), F32) + sum(_nbytes(a.shape, a.dtype) for a in (g1, wk, wqt, wvt, wuv, sgn, ws, bs_tile))
              + 4 * _nbytes((tm, QK_WIDTH), BF16))
    temps = _nbytes((tm, d), F32) + 6 * _nbytes((tm, 2 * SGU_WIDTH), F32)
    return pl.pallas_call(
        _in_proj_kernel, grid=(n // tm,), in_specs=in_specs, out_specs=out_specs, out_shape=out_shape,
        compiler_params=pltpu.CompilerParams(dimension_semantics=("arbitrary",),
                                             vmem_limit_bytes=_vmem_limit(blocks, temps)),
        name="in_proj",
    )(x2, g1, wk, wqt, wvt, wuv, sgn, ws, bs_tile)


def _attn_kernel(lam_ref, qt_ref, k_ref, vt_ref, band_ref, g_ref, o_ref, acc1_ref, acc2_ref, *, lam_init):
    t = qt_ref.shape[1]
    r = NEAR_ROWS
    i = pl.program_id(2)
    qt = qt_ref[...]
    row = lax.broadcasted_iota(jnp.int32, qt.shape, 0)
    zero = jnp.zeros_like(qt)
    q1 = jnp.where(row < HEAD_DIM, qt, zero)
    q2 = jnp.where(row >= HEAD_DIM, qt, zero)
    acc1_ref[...] = jnp.zeros_like(acc1_ref)
    acc2_ref[...] = jnp.zeros_like(acc2_ref)

    def add_prev_bias(s):
        return jnp.concatenate([s[:t - r], s[t - r:] + band_ref[:, t:2 * t]], axis=0)

    def add_diag_bias(s):
        return jnp.concatenate([s[r * a:r * (a + 1)] + band_ref[:, t - r * (a + 1):2 * t - r * (a + 1)]
                                for a in range(t // r)], axis=0)

    def block(start, add_bias, carry):
        kb = k_ref[pl.ds(start, t), :]
        vb = vt_ref[:, pl.ds(start, t)]
        new = []
        for qp, acc_ref, (m, l) in ((q1, acc1_ref, carry[0]), (q2, acc2_ref, carry[1])):
            s = jnp.dot(kb, qp, preferred_element_type=F32)
            if add_bias is not None:
                s = add_bias(s)
            m_new = jnp.maximum(m, jnp.max(s, axis=0, keepdims=True))
            alpha = jnp.exp(m - m_new)
            p = jnp.exp(s - m_new)
            l_new = alpha * l + jnp.sum(p, axis=0, keepdims=True)
            acc_ref[...] = alpha * acc_ref[...] + jnp.dot(vb, p.astype(BF16), preferred_element_type=F32)
            new.append((m_new, l_new))
        return tuple(new)

    init = (jnp.full((1, t), -jnp.inf, F32), jnp.zeros((1, t), F32))
    carry = (init, init)
    carry = lax.fori_loop(0, jnp.maximum(i - 1, 0),
                          lambda j, c: block(pl.multiple_of(j * t, t), None, c), carry)
    carry = lax.cond(i > 0, lambda c: block(pl.multiple_of((i - 1) * t, t), add_prev_bias, c),
                     lambda c: c, carry)
    (m1, l1), (m2, l2) = block(pl.multiple_of(i * t, t), add_diag_bias, carry)

    lp = lam_ref[...]
    lam = (jnp.exp(jnp.sum(lp[0:1] * lp[1:2], axis=-1, keepdims=True))
           - jnp.exp(jnp.sum(lp[2:3] * lp[3:4], axis=-1, keepdims=True)) + lam_init)
    o = acc1_ref[...] * (1.0 / l1) - lam * (acc2_ref[...] * (1.0 / l2))
    o = o * lax.rsqrt(jnp.mean(o * o, axis=0, keepdims=True) + NORM_EPS)
    o_ref[...] = ((o.T * g_ref[...]) * (1.0 - lam_init)).astype(BF16)


def _t5_bucket(rel):
    half = N_BUCKETS // 2
    ret = (rel > 0).astype(jnp.int32) * half
    n = jnp.abs(rel)
    max_exact = half // 2
    nf = jnp.maximum(n, 1).astype(F32)
    large = max_exact + (jnp.log(nf / max_exact) / math.log(MAX_DISTANCE / max_exact)
                         * (half - max_exact)).astype(jnp.int32)
    large = jnp.minimum(large, half - 1)
    return ret + jnp.where(n < max_exact, n, large)


def _bias_band(rel_bias, seq, t):
    r = NEAR_ROWS
    period = 2 * t + r
    buckets = jnp.arange(N_BUCKETS, dtype=jnp.int32)
    table = rel_bias.astype(F32)
    dist = t - 1 - jnp.arange(period, dtype=jnp.int32)
    onehot = _t5_bucket(dist)[:, None] == buckets[None, :]
    vec = jnp.sum(jnp.where(onehot[:, :, None], table[None], 0.0), axis=1)
    far = _t5_bucket(jnp.asarray(-(seq - 1), jnp.int32)) == buckets
    vec = (vec - jnp.sum(jnp.where(far[:, None], table, 0.0), axis=0)).T
    heads = vec.shape[0]
    band = jnp.tile(vec, (1, r))[:, :r * (period - 1)].reshape(heads, r, period - 1)[:, :, r - 1:r - 1 + 2 * t]
    a = jnp.arange(r, dtype=jnp.int32)[:, None]
    c = jnp.arange(2 * t, dtype=jnp.int32)[None, :]
    visible = (a // CHUNK) <= jnp.floor_divide(c - (t - r), CHUNK)
    return jnp.where(visible[None], band, -jnp.inf)


def _diff_attn(qt, k, vt, band, lam_rows, subln_g, batch, seq, lam_init):
    t = ATTN_TILE
    assert NEAR_ROWS >= MAX_DISTANCE and NEAR_ROWS % CHUNK == 0 and t % NEAR_ROWS == 0 and seq % t == 0
    n = batch * seq
    nq = seq // t
    in_specs = [pl.BlockSpec(lam_rows.shape, lambda b, h, i: (0, 0)),
                pl.BlockSpec((2 * HEAD_DIM, t), lambda b, h, i: (h, b * nq + i)),
                pl.BlockSpec((seq, 2 * HEAD_DIM), lambda b, h, i: (b, h)),
                pl.BlockSpec((V_DIM, seq), lambda b, h, i: (h, b)),
                pl.BlockSpec((None, NEAR_ROWS, 2 * t), lambda b, h, i: (h, 0, 0)),
                pl.BlockSpec(subln_g.shape, lambda b, h, i: (0, 0))]
    out_specs = pl.BlockSpec((t, V_DIM), lambda b, h, i: (b * nq + i, h))
    blocks = (_nbytes((2 * HEAD_DIM, t), BF16) + 2 * _nbytes((seq, 2 * HEAD_DIM), BF16)
              + _nbytes((NEAR_ROWS, 2 * t), F32) + _nbytes((t, V_DIM), BF16))
    temps = 2 * _nbytes((V_DIM, t), F32) + 8 * _nbytes((t, t), F32)
    return pl.pallas_call(
        functools.partial(_attn_kernel, lam_init=lam_init),
        grid=(batch, N_ATT_HEADS, nq), in_specs=in_specs, out_specs=out_specs,
        out_shape=jax.ShapeDtypeStruct((n, ATT_WIDTH), BF16),
        scratch_shapes=[pltpu.VMEM((V_DIM, t), F32), pltpu.VMEM((V_DIM, t), F32)],
        compiler_params=pltpu.CompilerParams(dimension_semantics=("arbitrary",) * 3,
                                             vmem_limit_bytes=_vmem_limit(blocks, temps)),
        name="diff_attn",
    )(lam_rows, qt, k, vt, band, subln_g)


def _merge_kernel(x_ref, g1_ref, attn_ref, sgu_ref, wg_ref, wpa_ref, wps_ref, wo_ref, h_ref):
    x = x_ref[...]
    d = x.shape[1]
    xn = (x * _rms_scale(x) * g1_ref[...]).astype(BF16)
    gate = jax.nn.sigmoid(jnp.dot(xn, wg_ref[...], preferred_element_type=F32))
    y_attn = jnp.dot(attn_ref[...], wpa_ref[...], preferred_element_type=F32)
    y_sgu = jnp.dot(sgu_ref[...], wps_ref[...], preferred_element_type=F32)
    merged = gate[:, :d] * y_attn + gate[:, d:] * y_sgu
    h_ref[...] = x + jnp.dot(merged.astype(BF16), wo_ref[...], preferred_element_type=F32)


def _merge(x2, g1, attn, sgu, wg, wpa, wps, wo):
    n, d = x2.shape
    tm = TOKEN_TILE
    full = lambda a: pl.BlockSpec(a.shape, lambda i: (0,) * a.ndim)
    row = lambda w: pl.BlockSpec((tm, w), lambda i: (i, 0))
    blocks = (2 * _nbytes((tm, d), F32) + 2 * _nbytes((tm, ATT_WIDTH), BF16)
              + sum(_nbytes(a.shape, a.dtype) for a in (g1, wg, wpa, wps, wo)))
    temps = 8 * _nbytes((tm, d), F32)
    return pl.pallas_call(
        _merge_kernel, grid=(n // tm,),
        in_specs=[row(d), full(g1), row(ATT_WIDTH), row(SGU_WIDTH), full(wg), full(wpa), full(wps), full(wo)],
        out_specs=row(d), out_shape=jax.ShapeDtypeStruct((n, d), F32),
        compiler_params=pltpu.CompilerParams(dimension_semantics=("arbitrary",),
                                             vmem_limit_bytes=_vmem_limit(blocks, temps)),
        name="merge",
    )(x2, g1, attn, sgu, wg, wpa, wps, wo)


def _ffn_kernel(h_ref, g2_ref, w1_ref, w2_ref, gf_ref, o_ref, *, final_norm):
    h = h_ref[...]
    hn = (h * _rms_scale(h) * g2_ref[...]).astype(BF16)
    d_ff = w1_ref.shape[1]
    out = h
    for c in range(d_ff // FF_CHUNK):
        lo = c * FF_CHUNK
        z = jnp.dot(hn, w1_ref[:, lo:lo + FF_CHUNK], preferred_element_type=F32)
        ff = jnp.square(jnp.maximum(z, 0.0)).astype(BF16)
        out = out + jnp.dot(ff, w2_ref[lo:lo + FF_CHUNK, :], preferred_element_type=F32)
    if final_norm:
        out = out * _rms_scale(out) * gf_ref[...]
    o_ref[...] = out


def _ffn(h, g2, w1, w2, gf, final_norm):
    n, d = h.shape
    tm = TOKEN_TILE
    full = lambda a: pl.BlockSpec(a.shape, lambda i: (0,) * a.ndim)
    row = pl.BlockSpec((tm, d), lambda i: (i, 0))
    blocks = 2 * _nbytes((tm, d), F32) + sum(_nbytes(a.shape, a.dtype) for a in (g2, w1, w2, gf))
    temps = 4 * _nbytes((tm, d), F32) + 3 * _nbytes((tm, FF_CHUNK), F32)
    return pl.pallas_call(
        functools.partial(_ffn_kernel, final_norm=final_norm), grid=(n // tm,),
        in_specs=[row, full(g2), full(w1), full(w2), full(gf)],
        out_specs=row, out_shape=jax.ShapeDtypeStruct((n, d), F32),
        compiler_params=pltpu.CompilerParams(dimension_semantics=("arbitrary",),
                                             vmem_limit_bytes=_vmem_limit(blocks, temps)),
        name="ffn",
    )(h, g2, w1, w2, gf)


def _layer(h, layer_idx, batch, seq, norm1_g, w_in, lam_rows, subln_g, bias, sgu_norm_g, w_spatial,
           b_spatial, w_proj_attn, w_proj_sgu, w_out, norm2_g, w_ff1, w_ff2, normf_g, final_norm):
    lam_init = 0.8 - 0.6 * math.exp(-0.3 * layer_idx)
    o_q, o_k, o_v, o_uv = QK_WIDTH, 2 * QK_WIDTH, 2 * QK_WIDTH + ATT_WIDTH, 2 * QK_WIDTH + ATT_WIDTH + 2 * SGU_WIDTH
    g1 = norm1_g[None, :]
    wqt = w_in[:, :o_q].T.astype(BF16)
    wk = w_in[:, o_q:o_k].astype(BF16)
    wvt = w_in[:, o_k:o_v].T.astype(BF16)
    wuv = w_in[:, o_v:o_uv].astype(BF16)
    wg = w_in[:, o_uv:].astype(BF16)
    bs_tile = jnp.repeat(b_spatial.T, SGU_GROUP_DIM, axis=1)
    k, qt, vt, sgu = _in_proj(h, g1, wk, wqt, wvt, wuv, sgu_norm_g, w_spatial, bs_tile)
    attn = _diff_attn(qt, k, vt, bias, lam_rows, subln_g[None, :], batch, seq, lam_init)
    h = _merge(h, g1, attn, sgu, wg, w_proj_attn.astype(BF16), w_proj_sgu.astype(BF16), w_out.astype(BF16))
    return _ffn(h, norm2_g[None, :], w_ff1.astype(BF16), w_ff2.astype(BF16), normf_g[None, :], final_norm)


def kernel(x, norm1_g, w_in, lam_q1, lam_k1, lam_q2, lam_k2, subln_g, rel_bias, sgu_norm_g, w_spatial,
           b_spatial, w_proj_attn, w_proj_sgu, w_out, norm2_g, w_ff1, w_ff2, normf_g):
    batch, seq, d = x.shape
    depth = w_in.shape[0]
    bias = _bias_band(rel_bias, seq, ATTN_TILE)
    h = x.reshape(batch * seq, d)
    for l in range(depth):
        lam_rows = jnp.stack([lam_q1[l], lam_k1[l], lam_q2[l], lam_k2[l]])
        h = _layer(h, l, batch, seq, norm1_g[l], w_in[l], lam_rows, subln_g[l], bias, sgu_norm_g[l],
                   w_spatial[l], b_spatial[l], w_proj_attn[l], w_proj_sgu[l], w_out[l], norm2_g[l],
                   w_ff1[l], w_ff2[l], normf_g, l == depth - 1)
    return h.reshape(batch, seq, d)
```

```python
import functools
import math

import jax
import jax.numpy as jnp
from jax import lax
from jax.experimental import pallas as pl
from jax.experimental.pallas import tpu as pltpu

F32 = jnp.float32
BF16 = jnp.bfloat16

CHUNK = 64
N_ATT_HEADS = 4
HEAD_DIM = 64
V_DIM = 2 * HEAD_DIM
QK_WIDTH = N_ATT_HEADS * 2 * HEAD_DIM
ATT_WIDTH = N_ATT_HEADS * V_DIM
N_SGU_GROUPS = 4
SGU_GROUP_DIM = 128
SGU_WIDTH = N_SGU_GROUPS * SGU_GROUP_DIM
SGU_CHUNK = 128
N_BUCKETS = 32
MAX_DISTANCE = 128
NORM_EPS = 1e-6
ATTN_SCALE = HEAD_DIM ** -0.5
LOG2_E = math.log2(math.e)
V_ROWS = V_DIM + 16

V7X_VMEM_BYTES = 64 * 1024 * 1024
V7X_LANES = 128

TOKEN_TILE = 512
ATTN_TILE = 1024
NEAR_ROWS = 128
FF_CHUNK = 1024

_NT_DIMS = (((1,), (1,)), ((), ()))


def _vmem_limit(block_bytes, temp_bytes):
    return int(min(2 * block_bytes + temp_bytes, V7X_VMEM_BYTES))


def _nbytes(shape, dtype):
    return math.prod(shape) * jnp.dtype(dtype).itemsize


def _rms_scale(x):
    return lax.rsqrt(jnp.mean(x * x, axis=-1, keepdims=True) + NORM_EPS)


def _in_proj_kernel(x_ref, g1_ref, wk_ref, wqt_ref, wvt_ref, wuv_ref, sgn_ref, ws_ref, bs_ref,
                    k_ref, qt_ref, vt_ref, sgu_ref):
    x = x_ref[...]
    xn = (x * _rms_scale(x) * g1_ref[...]).astype(BF16)
    k_ref[...] = jnp.dot(xn, wk_ref[...], preferred_element_type=F32).astype(BF16)
    qt = lax.dot_general(wqt_ref[...], xn, _NT_DIMS, preferred_element_type=F32)
    qt_ref[...] = (qt * (ATTN_SCALE * LOG2_E)).astype(BF16)
    vt = lax.dot_general(wvt_ref[...], xn, _NT_DIMS, preferred_element_type=F32).astype(BF16)
    ones = jnp.ones((V_ROWS - V_DIM, x.shape[0]), BF16)
    for h in range(N_ATT_HEADS):
        vt_ref[h * V_ROWS:h * V_ROWS + V_DIM, :] = vt[h * V_DIM:(h + 1) * V_DIM]
        vt_ref[h * V_ROWS + V_DIM:(h + 1) * V_ROWS, :] = ones

    zuv = jnp.dot(xn, wuv_ref[...], preferred_element_type=F32)
    guv = 0.5 * zuv * (1.0 + lax.erf(zuv * (2.0 ** -0.5)))
    tm = x.shape[0]
    t_idx = lax.broadcasted_iota(jnp.int32, (SGU_CHUNK, SGU_CHUNK), 0)
    s_idx = lax.broadcasted_iota(jnp.int32, (SGU_CHUNK, SGU_CHUNK), 1)
    for g in range(N_SGU_GROUPS):
        lo = g * SGU_GROUP_DIM
        u = guv[:, lo:lo + SGU_GROUP_DIM]
        v = guv[:, SGU_WIDTH + lo:SGU_WIDTH + lo + SGU_GROUP_DIM]
        vn = (v * _rms_scale(v) * sgn_ref[g:g + 1, :]).astype(BF16)
        w = jnp.where(s_idx <= t_idx, ws_ref[g], 0.0).astype(BF16)
        b = bs_ref[:, lo:lo + SGU_GROUP_DIM]
        for c in range(tm // SGU_CHUNK):
            r0 = c * SGU_CHUNK
            mixed = jnp.dot(w, vn[r0:r0 + SGU_CHUNK, :], preferred_element_type=F32) + b
            sgu_ref[r0:r0 + SGU_CHUNK, lo:lo + SGU_GROUP_DIM] = (u[r0:r0 + SGU_CHUNK, :] * mixed).astype(BF16)


def _in_proj(x2, g1, wk, wqt, wvt, wuv, sgn, ws, bs_tile):
    n, d = x2.shape
    tm = TOKEN_TILE
    full = lambda a: pl.BlockSpec(a.shape, lambda i: (0,) * a.ndim)
    in_specs = [pl.BlockSpec((tm, d), lambda i: (i, 0)),
                full(g1), full(wk), full(wqt), full(wvt), full(wuv), full(sgn), full(ws), full(bs_tile)]
    out_shape = (jax.ShapeDtypeStruct((n, QK_WIDTH), BF16),
                 jax.ShapeDtypeStruct((QK_WIDTH, n), BF16),
                 jax.ShapeDtypeStruct((N_ATT_HEADS * V_ROWS, n), BF16),
                 jax.ShapeDtypeStruct((n, SGU_WIDTH), BF16))
    out_specs = (pl.BlockSpec((tm, QK_WIDTH), lambda i: (i, 0)),
                 pl.BlockSpec((QK_WIDTH, tm), lambda i: (0, i)),
                 pl.BlockSpec((N_ATT_HEADS * V_ROWS, tm), lambda i: (0, i)),
                 pl.BlockSpec((tm, SGU_WIDTH), lambda i: (i, 0)))
    blocks = (_nbytes((tm, d), F32) + sum(_nbytes(a.shape, a.dtype) for a in (g1, wk, wqt, wvt, wuv, sgn, ws, bs_tile))
              + 4 * _nbytes((tm, QK_WIDTH), BF16))
    temps = _nbytes((tm, d), F32) + 6 * _nbytes((tm, 2 * SGU_WIDTH), F32)
    return pl.pallas_call(
        _in_proj_kernel, grid=(n // tm,), in_specs=in_specs, out_specs=out_specs, out_shape=out_shape,
        compiler_params=pltpu.CompilerParams(dimension_semantics=("arbitrary",),
                                             vmem_limit_bytes=_vmem_limit(blocks, temps)),
        name="in_proj",
    )(x2, g1, wk, wqt, wvt, wuv, sgn, ws, bs_tile)


def _attn_kernel(lam_ref, qt_ref, k_ref, vt_ref, band_ref, g_ref, o_ref, acc1_ref, acc2_ref, *, lam_init):
    t = qt_ref.shape[1]
    r = NEAR_ROWS
    i = pl.program_id(2)
    qt = qt_ref[...]
    row = lax.broadcasted_iota(jnp.int32, qt.shape, 0)
    zero = jnp.zeros_like(qt)
    q1 = jnp.where(row < HEAD_DIM, qt, zero)
    q2 = jnp.where(row >= HEAD_DIM, qt, zero)
    acc1_ref[...] = jnp.zeros_like(acc1_ref)
    acc2_ref[...] = jnp.zeros_like(acc2_ref)

    def add_prev_bias(s):
        return jnp.concatenate([s[:t - r], s[t - r:] + band_ref[:, t:2 * t]], axis=0)

    def add_diag_bias(s):
        return jnp.concatenate([s[r * a:r * (a + 1)] + band_ref[:, t - r * (a + 1):2 * t - r * (a + 1)]
                                for a in range(t // r)], axis=0)

    def block(start, add_bias, carry):
        kb = k_ref[pl.ds(start, t), :]
        vb = vt_ref[:, pl.ds(start, t)]
        new = []
        scores = [jnp.dot(kb, qp, preferred_element_type=F32) for qp in (q1, q2)]
        for s, acc_ref, m in ((scores[0], acc1_ref, carry[0]), (scores[1], acc2_ref, carry[1])):
            if add_bias is not None:
                s = add_bias(s)
            m_new = jnp.maximum(m, jnp.max(s, axis=0, keepdims=True))
            p = jnp.exp2(s - m_new).astype(BF16)
            acc_ref[...] = jnp.exp2(m - m_new) * acc_ref[...] + jnp.dot(vb, p, preferred_element_type=F32)
            new.append(m_new)
        return tuple(new)

    init = jnp.full((1, t), -jnp.inf, F32)
    carry = (init, init)
    n_far = jnp.maximum(i - 1, 0)
    carry = lax.fori_loop(0, n_far // 2,
                          lambda j, c: block(pl.multiple_of((2 * j + 1) * t, t), None,
                                             block(pl.multiple_of(2 * j * t, t), None, c)), carry)
    carry = lax.cond(n_far % 2 == 1, lambda c: block(pl.multiple_of((n_far - 1) * t, t), None, c),
                     lambda c: c, carry)
    carry = lax.cond(i > 0, lambda c: block(pl.multiple_of((i - 1) * t, t), add_prev_bias, c),
                     lambda c: c, carry)
    block(pl.multiple_of(i * t, t), add_diag_bias, carry)

    lp = lam_ref[...]
    lam = (jnp.exp(jnp.sum(lp[0:1] * lp[1:2], axis=-1, keepdims=True))
           - jnp.exp(jnp.sum(lp[2:3] * lp[3:4], axis=-1, keepdims=True)) + lam_init)
    o1 = acc1_ref[:V_DIM, :] * (1.0 / acc1_ref[V_DIM:V_DIM + 1, :])
    o2 = acc2_ref[:V_DIM, :] * (1.0 / acc2_ref[V_DIM:V_DIM + 1, :])
    o = o1 - lam * o2
    o = o * lax.rsqrt(jnp.mean(o * o, axis=0, keepdims=True) + NORM_EPS)
    o_ref[...] = ((o.T * g_ref[...]) * (1.0 - lam_init)).astype(BF16)


def _t5_bucket(rel):
    half = N_BUCKETS // 2
    ret = (rel > 0).astype(jnp.int32) * half
    n = jnp.abs(rel)
    max_exact = half // 2
    nf = jnp.maximum(n, 1).astype(F32)
    large = max_exact + (jnp.log(nf / max_exact) / math.log(MAX_DISTANCE / max_exact)
                         * (half - max_exact)).astype(jnp.int32)
    large = jnp.minimum(large, half - 1)
    return ret + jnp.where(n < max_exact, n, large)


def _bias_band(rel_bias, seq, t):
    r = NEAR_ROWS
    period = 2 * t + r
    buckets = jnp.arange(N_BUCKETS, dtype=jnp.int32)
    table = rel_bias.astype(F32)
    dist = t - 1 - jnp.arange(period, dtype=jnp.int32)
    onehot = _t5_bucket(dist)[:, None] == buckets[None, :]
    vec = jnp.sum(jnp.where(onehot[:, :, None], table[None], 0.0), axis=1)
    far = _t5_bucket(jnp.asarray(-(seq - 1), jnp.int32)) == buckets
    vec = (vec - jnp.sum(jnp.where(far[:, None], table, 0.0), axis=0)).T
    heads = vec.shape[0]
    band = jnp.tile(vec, (1, r))[:, :r * (period - 1)].reshape(heads, r, period - 1)[:, :, r - 1:r - 1 + 2 * t]
    a = jnp.arange(r, dtype=jnp.int32)[:, None]
    c = jnp.arange(2 * t, dtype=jnp.int32)[None, :]
    visible = (a // CHUNK) <= jnp.floor_divide(c - (t - r), CHUNK)
    return jnp.where(visible[None], band * LOG2_E, -jnp.inf)


def _diff_attn(qt, k, vt, band, lam_rows, subln_g, batch, seq, lam_init):
    t = ATTN_TILE
    assert NEAR_ROWS >= MAX_DISTANCE and NEAR_ROWS % CHUNK == 0 and t % NEAR_ROWS == 0 and seq % t == 0
    n = batch * seq
    nq = seq // t
    in_specs = [pl.BlockSpec(lam_rows.shape, lambda b, h, i: (0, 0)),
                pl.BlockSpec((2 * HEAD_DIM, t), lambda b, h, i: (h, b * nq + i)),
                pl.BlockSpec((seq, 2 * HEAD_DIM), lambda b, h, i: (b, h)),
                pl.BlockSpec((V_ROWS, seq), lambda b, h, i: (h, b)),
                pl.BlockSpec((None, NEAR_ROWS, 2 * t), lambda b, h, i: (h, 0, 0)),
                pl.BlockSpec(subln_g.shape, lambda b, h, i: (0, 0))]
    out_specs = pl.BlockSpec((t, V_DIM), lambda b, h, i: (b * nq + i, h))
    blocks = (_nbytes((2 * HEAD_DIM, t), BF16) + 2 * _nbytes((seq, 2 * HEAD_DIM), BF16)
              + _nbytes((NEAR_ROWS, 2 * t), F32) + _nbytes((t, V_DIM), BF16))
    temps = 2 * _nbytes((V_ROWS, t), F32) + 8 * _nbytes((t, t), F32)
    return pl.pallas_call(
        functools.partial(_attn_kernel, lam_init=lam_init),
        grid=(batch, N_ATT_HEADS, nq), in_specs=in_specs, out_specs=out_specs,
        out_shape=jax.ShapeDtypeStruct((n, ATT_WIDTH), BF16),
        scratch_shapes=[pltpu.VMEM((V_ROWS, t), F32), pltpu.VMEM((V_ROWS, t), F32)],
        compiler_params=pltpu.CompilerParams(dimension_semantics=("arbitrary",) * 3,
                                             vmem_limit_bytes=_vmem_limit(blocks, temps)),
        name="diff_attn",
    )(lam_rows, qt, k, vt, band, subln_g)


def _merge_kernel(x_ref, g1_ref, attn_ref, sgu_ref, wg_ref, wpa_ref, wps_ref, wo_ref, h_ref):
    x = x_ref[...]
    d = x.shape[1]
    xn = (x * _rms_scale(x) * g1_ref[...]).astype(BF16)
    gate = jax.nn.sigmoid(jnp.dot(xn, wg_ref[...], preferred_element_type=F32))
    y_attn = jnp.dot(attn_ref[...], wpa_ref[...], preferred_element_type=F32)
    y_sgu = jnp.dot(sgu_ref[...], wps_ref[...], preferred_element_type=F32)
    merged = gate[:, :d] * y_attn + gate[:, d:] * y_sgu
    h_ref[...] = x + jnp.dot(merged.astype(BF16), wo_ref[...], preferred_element_type=F32)


def _merge(x2, g1, attn, sgu, wg, wpa, wps, wo):
    n, d = x2.shape
    tm = TOKEN_TILE
    full = lambda a: pl.BlockSpec(a.shape, lambda i: (0,) * a.ndim)
    row = lambda w: pl.BlockSpec((tm, w), lambda i: (i, 0))
    blocks = (2 * _nbytes((tm, d), F32) + 2 * _nbytes((tm, ATT_WIDTH), BF16)
              + sum(_nbytes(a.shape, a.dtype) for a in (g1, wg, wpa, wps, wo)))
    temps = 8 * _nbytes((tm, d), F32)
    return pl.pallas_call(
        _merge_kernel, grid=(n // tm,),
        in_specs=[row(d), full(g1), row(ATT_WIDTH), row(SGU_WIDTH), full(wg), full(wpa), full(wps), full(wo)],
        out_specs=row(d), out_shape=jax.ShapeDtypeStruct((n, d), F32),
        compiler_params=pltpu.CompilerParams(dimension_semantics=("arbitrary",),
                                             vmem_limit_bytes=_vmem_limit(blocks, temps)),
        name="merge",
    )(x2, g1, attn, sgu, wg, wpa, wps, wo)


def _ffn_kernel(h_ref, g2_ref, w1_ref, w2_ref, gf_ref, o_ref, *, final_norm):
    h = h_ref[...]
    hn = (h * _rms_scale(h) * g2_ref[...]).astype(BF16)
    d_ff = w1_ref.shape[1]
    out = h
    for c in range(d_ff // FF_CHUNK):
        lo = c * FF_CHUNK
        z = jnp.dot(hn, w1_ref[:, lo:lo + FF_CHUNK], preferred_element_type=F32)
        ff = jnp.square(jnp.maximum(z, 0.0)).astype(BF16)
        out = out + jnp.dot(ff, w2_ref[lo:lo + FF_CHUNK, :], preferred_element_type=F32)
    if final_norm:
        out = out * _rms_scale(out) * gf_ref[...]
    o_ref[...] = out


def _ffn(h, g2, w1, w2, gf, final_norm):
    n, d = h.shape
    tm = TOKEN_TILE
    full = lambda a: pl.BlockSpec(a.shape, lambda i: (0,) * a.ndim)
    row = pl.BlockSpec((tm, d), lambda i: (i, 0))
    blocks = 2 * _nbytes((tm, d), F32) + sum(_nbytes(a.shape, a.dtype) for a in (g2, w1, w2, gf))
    temps = 4 * _nbytes((tm, d), F32) + 3 * _nbytes((tm, FF_CHUNK), F32)
    return pl.pallas_call(
        functools.partial(_ffn_kernel, final_norm=final_norm), grid=(n // tm,),
        in_specs=[row, full(g2), full(w1), full(w2), full(gf)],
        out_specs=row, out_shape=jax.ShapeDtypeStruct((n, d), F32),
        compiler_params=pltpu.CompilerParams(dimension_semantics=("arbitrary",),
                                             vmem_limit_bytes=_vmem_limit(blocks, temps)),
        name="ffn",
    )(h, g2, w1, w2, gf)


def _layer(h, layer_idx, batch, seq, norm1_g, w_in, lam_rows, subln_g, bias, sgu_norm_g, w_spatial,
           b_spatial, w_proj_attn, w_proj_sgu, w_out, norm2_g, w_ff1, w_ff2, normf_g, final_norm):
    lam_init = 0.8 - 0.6 * math.exp(-0.3 * layer_idx)
    o_q, o_k, o_v, o_uv = QK_WIDTH, 2 * QK_WIDTH, 2 * QK_WIDTH + ATT_WIDTH, 2 * QK_WIDTH + ATT_WIDTH + 2 * SGU_WIDTH
    g1 = norm1_g[None, :]
    wqt = w_in[:, :o_q].T.astype(BF16)
    wk = w_in[:, o_q:o_k].astype(BF16)
    wvt = w_in[:, o_k:o_v].T.astype(BF16)
    wuv = w_in[:, o_v:o_uv].astype(BF16)
    wg = w_in[:, o_uv:].astype(BF16)
    bs_tile = jnp.repeat(b_spatial.T, SGU_GROUP_DIM, axis=1)
    k, qt, vt, sgu = _in_proj(h, g1, wk, wqt, wvt, wuv, sgu_norm_g, w_spatial, bs_tile)
    attn = _diff_attn(qt, k, vt, bias, lam_rows, subln_g[None, :], batch, seq, lam_init)
    h = _merge(h, g1, attn, sgu, wg, w_proj_attn.astype(BF16), w_proj_sgu.astype(BF16), w_out.astype(BF16))
    return _ffn(h, norm2_g[None, :], w_ff1.astype(BF16), w_ff2.astype(BF16), normf_g[None, :], final_norm)


def kernel(x, norm1_g, w_in, lam_q1, lam_k1, lam_q2, lam_k2, subln_g, rel_bias, sgu_norm_g, w_spatial,
           b_spatial, w_proj_attn, w_proj_sgu, w_out, norm2_g, w_ff1, w_ff2, normf_g):
    batch, seq, d = x.shape
    depth = w_in.shape[0]
    bias = _bias_band(rel_bias, seq, ATTN_TILE)
    h = x.reshape(batch * seq, d)
    for l in range(depth):
        lam_rows = jnp.stack([lam_q1[l], lam_k1[l], lam_q2[l], lam_k2[l]])
        h = _layer(h, l, batch, seq, norm1_g[l], w_in[l], lam_rows, subln_g[l], bias, sgu_norm_g[l],
                   w_spatial[l], b_spatial[l], w_proj_attn[l], w_proj_sgu[l], w_out[l], norm2_g[l],
                   w_ff1[l], w_ff2[l], normf_g, l == depth - 1)
    return h.reshape(batch, seq, d)
```

```python
import functools
import math

import numpy as np

import jax
import jax.numpy as jnp
from jax import lax
from jax.experimental import pallas as pl
from jax.experimental.pallas import tpu as pltpu

F32 = jnp.float32
BF16 = jnp.bfloat16

CHUNK = 64
N_ATT_HEADS = 4
HEAD_DIM = 64
V_DIM = 2 * HEAD_DIM
QK_WIDTH = N_ATT_HEADS * 2 * HEAD_DIM
ATT_WIDTH = N_ATT_HEADS * V_DIM
N_SGU_GROUPS = 4
SGU_GROUP_DIM = 128
SGU_WIDTH = N_SGU_GROUPS * SGU_GROUP_DIM
SGU_CHUNK = 128
N_BUCKETS = 32
MAX_DISTANCE = 128
NORM_EPS = 1e-6
ATTN_SCALE = HEAD_DIM ** -0.5
LOG2_E = math.log2(math.e)
V_ROWS = V_DIM + 16

V7X_VMEM_BYTES = 64 * 1024 * 1024
V7X_LANES = 128

TOKEN_TILE = 512
ATTN_TILE = 512
ATTN_UNROLL = 4
FF_CHUNK = 1024

_NT_DIMS = (((1,), (1,)), ((), ()))


def _vmem_limit(block_bytes, temp_bytes):
    return int(min(2 * block_bytes + temp_bytes, V7X_VMEM_BYTES))


def _nbytes(shape, dtype):
    return math.prod(shape) * jnp.dtype(dtype).itemsize


def _rms_scale(x):
    return lax.rsqrt(jnp.mean(x * x, axis=-1, keepdims=True) + NORM_EPS)


def _in_proj_kernel(x_ref, g1_ref, wk_ref, wqt_ref, wvt_ref, wuv_ref, sgn_ref, ws_ref, bs_ref,
                    k_ref, qt_ref, vt_ref, sgu_ref):
    x = x_ref[...]
    xn = (x * _rms_scale(x) * g1_ref[...]).astype(BF16)
    k_ref[...] = jnp.dot(xn, wk_ref[...], preferred_element_type=F32).astype(BF16)
    qt = lax.dot_general(wqt_ref[...], xn, _NT_DIMS, preferred_element_type=F32)
    qt_ref[...] = (qt * (ATTN_SCALE * LOG2_E)).astype(BF16)
    vt = lax.dot_general(wvt_ref[...], xn, _NT_DIMS, preferred_element_type=F32).astype(BF16)
    ones = jnp.ones((V_ROWS - V_DIM, x.shape[0]), BF16)
    for h in range(N_ATT_HEADS):
        vt_ref[h * V_ROWS:h * V_ROWS + V_DIM, :] = vt[h * V_DIM:(h + 1) * V_DIM]
        vt_ref[h * V_ROWS + V_DIM:(h + 1) * V_ROWS, :] = ones

    zuv = jnp.dot(xn, wuv_ref[...], preferred_element_type=F32)
    guv = 0.5 * zuv * (1.0 + lax.erf(zuv * (2.0 ** -0.5)))
    tm = x.shape[0]
    t_idx = lax.broadcasted_iota(jnp.int32, (SGU_CHUNK, SGU_CHUNK), 0)
    s_idx = lax.broadcasted_iota(jnp.int32, (SGU_CHUNK, SGU_CHUNK), 1)
    for g in range(N_SGU_GROUPS):
        lo = g * SGU_GROUP_DIM
        u = guv[:, lo:lo + SGU_GROUP_DIM]
        v = guv[:, SGU_WIDTH + lo:SGU_WIDTH + lo + SGU_GROUP_DIM]
        vn = (v * _rms_scale(v) * sgn_ref[g:g + 1, :]).astype(BF16)
        w = jnp.where(s_idx <= t_idx, ws_ref[g], 0.0).astype(BF16)
        b = bs_ref[:, lo:lo + SGU_GROUP_DIM]
        for c in range(tm // SGU_CHUNK):
            r0 = c * SGU_CHUNK
            mixed = jnp.dot(w, vn[r0:r0 + SGU_CHUNK, :], preferred_element_type=F32) + b
            sgu_ref[r0:r0 + SGU_CHUNK, lo:lo + SGU_GROUP_DIM] = (u[r0:r0 + SGU_CHUNK, :] * mixed).astype(BF16)


def _in_proj(x2, g1, wk, wqt, wvt, wuv, sgn, ws, bs_tile):
    n, d = x2.shape
    tm = TOKEN_TILE
    full = lambda a: pl.BlockSpec(a.shape, lambda i: (0,) * a.ndim)
    in_specs = [pl.BlockSpec((tm, d), lambda i: (i, 0)),
                full(g1), full(wk), full(wqt), full(wvt), full(wuv), full(sgn), full(ws), full(bs_tile)]
    out_shape = (jax.ShapeDtypeStruct((n, QK_WIDTH), BF16),
                 jax.ShapeDtypeStruct((QK_WIDTH, n), BF16),
                 jax.ShapeDtypeStruct((N_ATT_HEADS * V_ROWS, n), BF16),
                 jax.ShapeDtypeStruct((n, SGU_WIDTH), BF16))
    out_specs = (pl.BlockSpec((tm, QK_WIDTH), lambda i: (i, 0)),
                 pl.BlockSpec((QK_WIDTH, tm), lambda i: (0, i)),
                 pl.BlockSpec((N_ATT_HEADS * V_ROWS, tm), lambda i: (0, i)),
                 pl.BlockSpec((tm, SGU_WIDTH), lambda i: (i, 0)))
    blocks = (_nbytes((tm, d), F32) + sum(_nbytes(a.shape, a.dtype) for a in (g1, wk, wqt, wvt, wuv, sgn, ws, bs_tile))
              + 3 * _nbytes((tm, QK_WIDTH), BF16) + _nbytes((N_ATT_HEADS * V_ROWS, tm), BF16))
    temps = _nbytes((tm, d), F32) + 6 * _nbytes((tm, 2 * SGU_WIDTH), F32)
    return pl.pallas_call(
        _in_proj_kernel, grid=(n // tm,), in_specs=in_specs, out_specs=out_specs, out_shape=out_shape,
        compiler_params=pltpu.CompilerParams(dimension_semantics=("arbitrary",),
                                             vmem_limit_bytes=_vmem_limit(blocks, temps)),
        name="in_proj",
    )(x2, g1, wk, wqt, wvt, wuv, sgn, ws, bs_tile)


_KIND_DIAG, _KIND_PREV, _KIND_FAR, _KIND_NONE = 0, 1, 2, 3


def _attn_schedule(nq):
    far = [(i, j, _KIND_FAR) for i in range(nq) for j in range(i - 1)]
    n_far = len(far) - len(far) % ATTN_UNROLL
    near = far[n_far:] + [(i, j, _KIND_DIAG if j == i else _KIND_PREV) for i in range(nq) for j in range(max(i - 1, 0), i + 1)]
    near += [(nq - 1, 0, _KIND_NONE)] * (-len(near) % ATTN_UNROLL)
    seen, steps = set(), []
    for i, j, kind in far[:n_far] + near:
        steps.append((i, j, kind, int(i not in seen)))
        seen.add(i)
    return np.asarray(steps, np.int32).T, n_far


def _attn_kernel(sched_ref, lam_ref, qt_ref, k_ref, vt_ref, bias_ref, g_ref, o_ref, s_ref, acc_ref, m_ref,
                 *, lam_init, n_far, n_steps):
    t = ATTN_TILE
    nq = acc_ref.shape[0]
    row_id = lax.broadcasted_iota(jnp.int32, (2 * HEAD_DIM, t), 0)
    acc_ref[...] = jnp.zeros_like(acc_ref)
    m_ref[...] = jnp.zeros_like(m_ref)

    def scores(n, slot, with_bias):
        i, j, kind = sched_ref[0, n], sched_ref[1, n], sched_ref[2, n]
        qt = qt_ref[:, pl.ds(pl.multiple_of(i * t, t), t)]
        kb = k_ref[pl.ds(pl.multiple_of(j * t, t), t), :]
        zero = jnp.zeros_like(qt)
        block_max = []
        for mp, keep in enumerate((row_id < HEAD_DIM, row_id >= HEAD_DIM)):
            qp = jnp.where(keep, qt, zero)
            s = jnp.dot(kb, qp, preferred_element_type=F32)
            if with_bias:
                s = s + bias_ref[:, pl.ds(pl.multiple_of(kind * t, t), t)]
            s_ref[slot, mp] = s
            block_max.append(jnp.max(s, axis=0, keepdims=True))
        return tuple(block_max)

    def finish(n, slot, block_max):
        i, j, first = sched_ref[0, n], sched_ref[1, n], sched_ref[3, n]
        vb = vt_ref[:, pl.ds(pl.multiple_of(j * t, t), t)]
        for mp in range(2):
            m_old = jnp.where(first == 1, -jnp.inf, m_ref[i, mp])
            m_new = jnp.maximum(m_old, block_max[mp])
            p = jnp.exp2(s_ref[slot, mp] - m_new).astype(BF16)
            acc_ref[i, mp] = (jnp.exp2(m_old - m_new) * acc_ref[i, mp]
                              + jnp.dot(vb, p, preferred_element_type=F32))
            m_ref[i, mp] = m_new

    def phase(lo, hi, with_bias):
        def body(it, block_max):
            for u in range(ATTN_UNROLL):
                n = lo + it * ATTN_UNROLL + u
                next_max = scores(jnp.minimum(n + 1, hi - 1), (u + 1) % 2, with_bias)
                finish(n, u % 2, block_max)
                block_max = next_max
            return block_max
        if hi > lo:
            lax.fori_loop(0, (hi - lo) // ATTN_UNROLL, body, scores(lo, 0, with_bias))

    phase(0, n_far, False)
    phase(n_far, n_steps, True)

    lp = lam_ref[...]
    lam = (jnp.exp(jnp.sum(lp[0:1] * lp[1:2], axis=-1, keepdims=True))
           - jnp.exp(jnp.sum(lp[2:3] * lp[3:4], axis=-1, keepdims=True)) + lam_init)
    gain = g_ref[...]

    def normalise(i, _):
        heads = [acc_ref[i, mp, :V_DIM, :] * (1.0 / acc_ref[i, mp, V_DIM:V_DIM + 1, :]) for mp in range(2)]
        o = heads[0] - lam * heads[1]
        o = o * lax.rsqrt(jnp.mean(o * o, axis=0, keepdims=True) + NORM_EPS)
        o_ref[pl.ds(pl.multiple_of(i * t, t), t), :] = ((o.T * gain) * (1.0 - lam_init)).astype(BF16)
        return 0

    lax.fori_loop(0, nq, normalise, 0)


def _t5_bucket(rel):
    half = N_BUCKETS // 2
    ret = (rel > 0).astype(jnp.int32) * half
    n = jnp.abs(rel)
    max_exact = half // 2
    nf = jnp.maximum(n, 1).astype(F32)
    large = max_exact + (jnp.log(nf / max_exact) / math.log(MAX_DISTANCE / max_exact)
                         * (half - max_exact)).astype(jnp.int32)
    large = jnp.minimum(large, half - 1)
    return ret + jnp.where(n < max_exact, n, large)


def _bias_tiles(rel_bias, seq, t):
    period = 3 * t
    buckets = jnp.arange(N_BUCKETS, dtype=jnp.int32)
    table = rel_bias.astype(F32)
    dist = t - 1 - jnp.arange(period, dtype=jnp.int32)
    onehot = _t5_bucket(dist)[:, None] == buckets[None, :]
    vec = jnp.sum(jnp.where(onehot[:, :, None], table[None], 0.0), axis=1)
    far = _t5_bucket(jnp.asarray(-(seq - 1), jnp.int32)) == buckets
    vec = (vec - jnp.sum(jnp.where(far[:, None], table, 0.0), axis=0)).T
    heads = vec.shape[0]
    near = jnp.tile(vec, (1, t))[:, :t * (period - 1)].reshape(heads, t, period - 1)[:, :, t - 1:3 * t - 1]
    a = jnp.arange(t, dtype=jnp.int32)[:, None]
    c = jnp.arange(2 * t, dtype=jnp.int32)[None, :]
    visible = (a // CHUNK) <= (c // CHUNK)
    near = jnp.where(visible[None], near * LOG2_E, -jnp.inf)
    return jnp.concatenate([near, jnp.zeros((heads, t, t), F32), jnp.full((heads, t, t), -jnp.inf, F32)], axis=-1)


def _diff_attn(qt, k, vt, bias, lam_rows, subln_g, batch, seq, lam_init):
    t = ATTN_TILE
    assert t >= MAX_DISTANCE and t % CHUNK == 0 and seq % t == 0
    n = batch * seq
    nq = seq // t
    sched, n_far = _attn_schedule(nq)
    n_steps = sched.shape[1]
    assert ATTN_UNROLL % 2 == 0 and n_far % ATTN_UNROLL == 0 and n_steps % ATTN_UNROLL == 0
    in_specs = [pl.BlockSpec(lam_rows.shape, lambda b, h, s: (0, 0)),
                pl.BlockSpec((2 * HEAD_DIM, seq), lambda b, h, s: (h, b)),
                pl.BlockSpec((seq, 2 * HEAD_DIM), lambda b, h, s: (b, h)),
                pl.BlockSpec((V_ROWS, seq), lambda b, h, s: (h, b)),
                pl.BlockSpec((None, t, 4 * t), lambda b, h, s: (h, 0, 0)),
                pl.BlockSpec(subln_g.shape, lambda b, h, s: (0, 0))]
    out_specs = pl.BlockSpec((seq, V_DIM), lambda b, h, s: (b, h))
    blocks = (3 * _nbytes((seq, 2 * HEAD_DIM), BF16) + _nbytes((V_ROWS, seq), BF16) + _nbytes((t, 4 * t), F32))
    scratch = [pltpu.VMEM((2, 2, t, t), F32),
               pltpu.VMEM((nq, 2, V_ROWS, t), F32),
               pltpu.VMEM((nq, 2, 1, t), F32)]
    temps = sum(_nbytes(s.shape, s.dtype) for s in scratch) + 8 * _nbytes((t, t), F32)
    return pl.pallas_call(
        functools.partial(_attn_kernel, lam_init=lam_init, n_far=n_far, n_steps=n_steps),
        grid_spec=pltpu.PrefetchScalarGridSpec(
            num_scalar_prefetch=1, grid=(batch, N_ATT_HEADS), in_specs=in_specs, out_specs=out_specs,
            scratch_shapes=scratch),
        out_shape=jax.ShapeDtypeStruct((n, ATT_WIDTH), BF16),
        compiler_params=pltpu.CompilerParams(dimension_semantics=("arbitrary",) * 2,
                                             vmem_limit_bytes=_vmem_limit(blocks, temps)),
        name="diff_attn",
    )(jnp.asarray(sched), lam_rows, qt, k, vt, bias, subln_g)


def _merge_kernel(x_ref, g1_ref, attn_ref, sgu_ref, wg_ref, wpa_ref, wps_ref, wo_ref, h_ref):
    x = x_ref[...]
    d = x.shape[1]
    xn = (x * _rms_scale(x) * g1_ref[...]).astype(BF16)
    gate = jax.nn.sigmoid(jnp.dot(xn, wg_ref[...], preferred_element_type=F32))
    y_attn = jnp.dot(attn_ref[...], wpa_ref[...], preferred_element_type=F32)
    y_sgu = jnp.dot(sgu_ref[...], wps_ref[...], preferred_element_type=F32)
    merged = gate[:, :d] * y_attn + gate[:, d:] * y_sgu
    h_ref[...] = x + jnp.dot(merged.astype(BF16), wo_ref[...], preferred_element_type=F32)


def _merge(x2, g1, attn, sgu, wg, wpa, wps, wo):
    n, d = x2.shape
    tm = TOKEN_TILE
    full = lambda a: pl.BlockSpec(a.shape, lambda i: (0,) * a.ndim)
    row = lambda w: pl.BlockSpec((tm, w), lambda i: (i, 0))
    blocks = (2 * _nbytes((tm, d), F32) + 2 * _nbytes((tm, ATT_WIDTH), BF16)
              + sum(_nbytes(a.shape, a.dtype) for a in (g1, wg, wpa, wps, wo)))
    temps = 8 * _nbytes((tm, d), F32)
    return pl.pallas_call(
        _merge_kernel, grid=(n // tm,),
        in_specs=[row(d), full(g1), row(ATT_WIDTH), row(SGU_WIDTH), full(wg), full(wpa), full(wps), full(wo)],
        out_specs=row(d), out_shape=jax.ShapeDtypeStruct((n, d), F32),
        compiler_params=pltpu.CompilerParams(dimension_semantics=("arbitrary",),
                                             vmem_limit_bytes=_vmem_limit(blocks, temps)),
        name="merge",
    )(x2, g1, attn, sgu, wg, wpa, wps, wo)


def _ffn_kernel(h_ref, g2_ref, w1_ref, w2_ref, gf_ref, o_ref, *, final_norm):
    h = h_ref[...]
    hn = (h * _rms_scale(h) * g2_ref[...]).astype(BF16)
    d_ff = w1_ref.shape[1]
    out = h
    for c in range(d_ff // FF_CHUNK):
        lo = c * FF_CHUNK
        z = jnp.dot(hn, w1_ref[:, lo:lo + FF_CHUNK], preferred_element_type=F32)
        ff = jnp.square(jnp.maximum(z, 0.0)).astype(BF16)
        out = out + jnp.dot(ff, w2_ref[lo:lo + FF_CHUNK, :], preferred_element_type=F32)
    if final_norm:
        out = out * _rms_scale(out) * gf_ref[...]
    o_ref[...] = out


def _ffn(h, g2, w1, w2, gf, final_norm):
    n, d = h.shape
    tm = TOKEN_TILE
    full = lambda a: pl.BlockSpec(a.shape, lambda i: (0,) * a.ndim)
    row = pl.BlockSpec((tm, d), lambda i: (i, 0))
    blocks = 2 * _nbytes((tm, d), F32) + sum(_nbytes(a.shape, a.dtype) for a in (g2, w1, w2, gf))
    temps = 4 * _nbytes((tm, d), F32) + 3 * _nbytes((tm, FF_CHUNK), F32)
    return pl.pallas_call(
        functools.partial(_ffn_kernel, final_norm=final_norm), grid=(n // tm,),
        in_specs=[row, full(g2), full(w1), full(w2), full(gf)],
        out_specs=row, out_shape=jax.ShapeDtypeStruct((n, d), F32),
        compiler_params=pltpu.CompilerParams(dimension_semantics=("arbitrary",),
                                             vmem_limit_bytes=_vmem_limit(blocks, temps)),
        name="ffn",
    )(h, g2, w1, w2, gf)


def _layer(h, layer_idx, batch, seq, norm1_g, w_in, lam_rows, subln_g, bias, sgu_norm_g, w_spatial,
           b_spatial, w_proj_attn, w_proj_sgu, w_out, norm2_g, w_ff1, w_ff2, normf_g, final_norm):
    lam_init = 0.8 - 0.6 * math.exp(-0.3 * layer_idx)
    o_q, o_k, o_v, o_uv = QK_WIDTH, 2 * QK_WIDTH, 2 * QK_WIDTH + ATT_WIDTH, 2 * QK_WIDTH + ATT_WIDTH + 2 * SGU_WIDTH
    g1 = norm1_g[None, :]
    wqt = w_in[:, :o_q].T.astype(BF16)
    wk = w_in[:, o_q:o_k].astype(BF16)
    wvt = w_in[:, o_k:o_v].T.astype(BF16)
    wuv = w_in[:, o_v:o_uv].astype(BF16)
    wg = w_in[:, o_uv:].astype(BF16)
    bs_tile = jnp.repeat(b_spatial.T, SGU_GROUP_DIM, axis=1)
    k, qt, vt, sgu = _in_proj(h, g1, wk, wqt, wvt, wuv, sgu_norm_g, w_spatial, bs_tile)
    attn = _diff_attn(qt, k, vt, bias, lam_rows, subln_g[None, :], batch, seq, lam_init)
    h = _merge(h, g1, attn, sgu, wg, w_proj_attn.astype(BF16), w_proj_sgu.astype(BF16), w_out.astype(BF16))
    return _ffn(h, norm2_g[None, :], w_ff1.astype(BF16), w_ff2.astype(BF16), normf_g[None, :], final_norm)


def kernel(x, norm1_g, w_in, lam_q1, lam_k1, lam_q2, lam_k2, subln_g, rel_bias, sgu_norm_g, w_spatial,
           b_spatial, w_proj_attn, w_proj_sgu, w_out, norm2_g, w_ff1, w_ff2, normf_g):
    batch, seq, d = x.shape
    depth = w_in.shape[0]
    bias = _bias_tiles(rel_bias, seq, ATTN_TILE)
    h = x.reshape(batch * seq, d)
    for l in range(depth):
        lam_rows = jnp.stack([lam_q1[l], lam_k1[l], lam_q2[l], lam_k2[l]])
        h = _layer(h, l, batch, seq, norm1_g[l], w_in[l], lam_rows, subln_g[l], bias, sgu_norm_g[l],
                   w_spatial[l], b_spatial[l], w_proj_attn[l], w_proj_sgu[l], w_out[l], norm2_g[l],
                   w_ff1[l], w_ff2[l], normf_g, l == depth - 1)
    return h.reshape(batch, seq, d)
```

```python
import functools
import math

import numpy as np

import jax
import jax.numpy as jnp
from jax import lax
from jax.experimental import pallas as pl
from jax.experimental.pallas import tpu as pltpu

F32 = jnp.float32
BF16 = jnp.bfloat16

CHUNK = 64
N_ATT_HEADS = 4
HEAD_DIM = 64
V_DIM = 2 * HEAD_DIM
QK_WIDTH = N_ATT_HEADS * 2 * HEAD_DIM
ATT_WIDTH = N_ATT_HEADS * V_DIM
N_SGU_GROUPS = 4
SGU_GROUP_DIM = 128
SGU_WIDTH = N_SGU_GROUPS * SGU_GROUP_DIM
SGU_CHUNK = 128
N_BUCKETS = 32
MAX_DISTANCE = 128
NORM_EPS = 1e-6
ATTN_SCALE = HEAD_DIM ** -0.5
LOG2_E = math.log2(math.e)
V_ROWS = V_DIM + 16

V7X_VMEM_BYTES = 64 * 1024 * 1024
V7X_LANES = 128

TOKEN_TILE = 512
ATTN_TILE = 512
ATTN_UNROLL = 8
BIAS_SUB = 128
FF_CHUNK = 1024

_NT_DIMS = (((1,), (1,)), ((), ()))


def _vmem_limit(block_bytes, temp_bytes):
    return int(min(2 * block_bytes + temp_bytes, V7X_VMEM_BYTES))


def _nbytes(shape, dtype):
    return math.prod(shape) * jnp.dtype(dtype).itemsize


def _rms_scale(x):
    return lax.rsqrt(jnp.mean(x * x, axis=-1, keepdims=True) + NORM_EPS)


def _in_proj_kernel(x_ref, g1_ref, wk_ref, wqt_ref, wvt_ref, wuv_ref, sgn_ref, ws_ref, bs_ref,
                    k_ref, qt_ref, vt_ref, sgu_ref):
    x = x_ref[...]
    xn = (x * _rms_scale(x) * g1_ref[...]).astype(BF16)
    k_ref[...] = jnp.dot(xn, wk_ref[...], preferred_element_type=F32).astype(BF16)
    qt = lax.dot_general(wqt_ref[...], xn, _NT_DIMS, preferred_element_type=F32)
    qt_ref[...] = (qt * (ATTN_SCALE * LOG2_E)).astype(BF16)
    vt = lax.dot_general(wvt_ref[...], xn, _NT_DIMS, preferred_element_type=F32).astype(BF16)
    ones = jnp.ones((V_ROWS - V_DIM, x.shape[0]), BF16)
    for h in range(N_ATT_HEADS):
        vt_ref[h * V_ROWS:h * V_ROWS + V_DIM, :] = vt[h * V_DIM:(h + 1) * V_DIM]
        vt_ref[h * V_ROWS + V_DIM:(h + 1) * V_ROWS, :] = ones

    zuv = jnp.dot(xn, wuv_ref[...], preferred_element_type=F32)
    guv = 0.5 * zuv * (1.0 + lax.erf(zuv * (2.0 ** -0.5)))
    tm = x.shape[0]
    t_idx = lax.broadcasted_iota(jnp.int32, (SGU_CHUNK, SGU_CHUNK), 0)
    s_idx = lax.broadcasted_iota(jnp.int32, (SGU_CHUNK, SGU_CHUNK), 1)
    for g in range(N_SGU_GROUPS):
        lo = g * SGU_GROUP_DIM
        u = guv[:, lo:lo + SGU_GROUP_DIM]
        v = guv[:, SGU_WIDTH + lo:SGU_WIDTH + lo + SGU_GROUP_DIM]
        vn = (v * _rms_scale(v) * sgn_ref[g:g + 1, :]).astype(BF16)
        w = jnp.where(s_idx <= t_idx, ws_ref[g], 0.0).astype(BF16)
        b = bs_ref[:, lo:lo + SGU_GROUP_DIM]
        for c in range(tm // SGU_CHUNK):
            r0 = c * SGU_CHUNK
            mixed = jnp.dot(w, vn[r0:r0 + SGU_CHUNK, :], preferred_element_type=F32) + b
            sgu_ref[r0:r0 + SGU_CHUNK, lo:lo + SGU_GROUP_DIM] = (u[r0:r0 + SGU_CHUNK, :] * mixed).astype(BF16)


def _in_proj(x2, g1, wk, wqt, wvt, wuv, sgn, ws, bs_tile):
    n, d = x2.shape
    tm = TOKEN_TILE
    full = lambda a: pl.BlockSpec(a.shape, lambda i: (0,) * a.ndim)
    in_specs = [pl.BlockSpec((tm, d), lambda i: (i, 0)),
                full(g1), full(wk), full(wqt), full(wvt), full(wuv), full(sgn), full(ws), full(bs_tile)]
    out_shape = (jax.ShapeDtypeStruct((n, QK_WIDTH), BF16),
                 jax.ShapeDtypeStruct((QK_WIDTH, n), BF16),
                 jax.ShapeDtypeStruct((N_ATT_HEADS * V_ROWS, n), BF16),
                 jax.ShapeDtypeStruct((n, SGU_WIDTH), BF16))
    out_specs = (pl.BlockSpec((tm, QK_WIDTH), lambda i: (i, 0)),
                 pl.BlockSpec((QK_WIDTH, tm), lambda i: (0, i)),
                 pl.BlockSpec((N_ATT_HEADS * V_ROWS, tm), lambda i: (0, i)),
                 pl.BlockSpec((tm, SGU_WIDTH), lambda i: (i, 0)))
    blocks = (_nbytes((tm, d), F32) + sum(_nbytes(a.shape, a.dtype) for a in (g1, wk, wqt, wvt, wuv, sgn, ws, bs_tile))
              + 3 * _nbytes((tm, QK_WIDTH), BF16) + _nbytes((N_ATT_HEADS * V_ROWS, tm), BF16))
    temps = _nbytes((tm, d), F32) + 6 * _nbytes((tm, 2 * SGU_WIDTH), F32)
    return pl.pallas_call(
        _in_proj_kernel, grid=(n // tm,), in_specs=in_specs, out_specs=out_specs, out_shape=out_shape,
        compiler_params=pltpu.CompilerParams(dimension_semantics=("arbitrary",),
                                             vmem_limit_bytes=_vmem_limit(blocks, temps)),
        name="in_proj",
    )(x2, g1, wk, wqt, wvt, wuv, sgn, ws, bs_tile)


_KIND_DIAG, _KIND_PREV, _KIND_FAR, _KIND_NONE = 0, 1, 2, 3


def _attn_schedule(nq):
    far = [(i, j, _KIND_FAR) for i in range(nq) for j in range(i - 1)]
    n_far = len(far) - len(far) % ATTN_UNROLL
    near = far[n_far:] + [(i, j, _KIND_DIAG if j == i else _KIND_PREV) for i in range(nq) for j in range(max(i - 1, 0), i + 1)]
    near += [(nq - 1, 0, _KIND_NONE)] * (-len(near) % ATTN_UNROLL)
    seen, steps = set(), []
    for i, j, kind in far[:n_far] + near:
        steps.append((i, j, kind, int(i not in seen)))
        seen.add(i)
    return np.asarray(steps, np.int32).T, n_far


def _attn_kernel(sched_ref, lam_ref, qt_ref, k_ref, vt_ref, bias_ref, g_ref, o_ref, s_ref, acc_ref, m_ref,
                 *, lam_init, n_far, n_steps):
    t = ATTN_TILE
    nq = acc_ref.shape[0]
    row_id = lax.broadcasted_iota(jnp.int32, (2 * HEAD_DIM, t), 0)
    acc_ref[...] = jnp.zeros_like(acc_ref)
    m_ref[...] = jnp.zeros_like(m_ref)

    def scores(n, slot, with_bias):
        i, j, kind = sched_ref[0, n], sched_ref[1, n], sched_ref[2, n]
        qt = qt_ref[:, pl.ds(pl.multiple_of(i * t, t), t)]
        kb = k_ref[pl.ds(pl.multiple_of(j * t, t), t), :]
        zero = jnp.zeros_like(qt)
        block_max = []
        for mp, keep in enumerate((row_id < HEAD_DIM, row_id >= HEAD_DIM)):
            qp = jnp.where(keep, qt, zero)
            s = jnp.dot(kb, qp, preferred_element_type=F32)
            if with_bias:
                s = s + bias_ref[:, pl.ds(pl.multiple_of(kind * t, t), t)]
            s_ref[slot, mp] = s
            block_max.append(jnp.max(s, axis=0, keepdims=True))
        return tuple(block_max)

    def finish(n, slot, block_max):
        i, j, first = sched_ref[0, n], sched_ref[1, n], sched_ref[3, n]
        vb = vt_ref[:, pl.ds(pl.multiple_of(j * t, t), t)]
        for mp in range(2):
            m_old = jnp.where(first == 1, -jnp.inf, m_ref[i, mp])
            m_new = jnp.maximum(m_old, block_max[mp])
            p = jnp.exp2(s_ref[slot, mp] - m_new).astype(BF16)
            acc_ref[i, mp] = (jnp.exp2(m_old - m_new) * acc_ref[i, mp]
                              + jnp.dot(vb, p, preferred_element_type=F32))
            m_ref[i, mp] = m_new

    def phase(lo, hi, with_bias):
        def body(it, block_max):
            for u in range(ATTN_UNROLL):
                n = lo + it * ATTN_UNROLL + u
                next_max = scores(jnp.minimum(n + 1, hi - 1), (u + 1) % 2, with_bias)
                finish(n, u % 2, block_max)
                block_max = next_max
            return block_max
        if hi > lo:
            lax.fori_loop(0, (hi - lo) // ATTN_UNROLL, body, scores(lo, 0, with_bias))

    phase(0, n_far, False)
    phase(n_far, n_steps, True)

    lp = lam_ref[...]
    lam = (jnp.exp(jnp.sum(lp[0:1] * lp[1:2], axis=-1, keepdims=True))
           - jnp.exp(jnp.sum(lp[2:3] * lp[3:4], axis=-1, keepdims=True)) + lam_init)
    gain = g_ref[...]

    def normalise(i, _):
        heads = [acc_ref[i, mp, :V_DIM, :] * (1.0 / acc_ref[i, mp, V_DIM:V_DIM + 1, :]) for mp in range(2)]
        o = heads[0] - lam * heads[1]
        o = o * lax.rsqrt(jnp.mean(o * o, axis=0, keepdims=True) + NORM_EPS)
        o_ref[pl.ds(pl.multiple_of(i * t, t), t), :] = ((o.T * gain) * (1.0 - lam_init)).astype(BF16)
        return 0

    lax.fori_loop(0, nq, normalise, 0)


def _t5_bucket(rel):
    half = N_BUCKETS // 2
    ret = (rel > 0).astype(jnp.int32) * half
    n = jnp.abs(rel)
    max_exact = half // 2
    nf = jnp.maximum(n, 1).astype(F32)
    large = max_exact + (jnp.log(nf / max_exact) / math.log(MAX_DISTANCE / max_exact)
                         * (half - max_exact)).astype(jnp.int32)
    large = jnp.minimum(large, half - 1)
    return ret + jnp.where(n < max_exact, n, large)


def _bias_tiles(rel_bias, seq, t):
    r = BIAS_SUB
    period = 3 * r
    buckets = jnp.arange(N_BUCKETS, dtype=jnp.int32)
    table = rel_bias.astype(F32)
    dist = r - 1 - jnp.arange(period, dtype=jnp.int32)
    onehot = _t5_bucket(dist)[:, None] == buckets[None, :]
    vec = jnp.sum(jnp.where(onehot[:, :, None], table[None], 0.0), axis=1)
    far = _t5_bucket(jnp.asarray(-(seq - 1), jnp.int32)) == buckets
    vec = (vec - jnp.sum(jnp.where(far[:, None], table, 0.0), axis=0)).T
    heads = vec.shape[0]
    band = jnp.tile(vec, (1, r))[:, :r * (period - 1)].reshape(heads, r, period - 1)[:, :, r - 1:3 * r - 1]
    a = jnp.arange(r, dtype=jnp.int32)[:, None]
    c = jnp.arange(r, dtype=jnp.int32)[None, :]
    on_diag = jnp.where(((a // CHUNK) <= (c // CHUNK))[None], band[:, :, :r] * LOG2_E, -jnp.inf)
    below = band[:, :, r:] * LOG2_E
    zero = jnp.zeros_like(below)
    hidden = jnp.full_like(below, -jnp.inf)
    nb = t // r

    def sub_tile(key_blk, query_blk):
        d = key_blk - query_blk
        return hidden if d > 0 else on_diag if d == 0 else below if d == -1 else zero

    rows = [jnp.concatenate([sub_tile(kb, qb) for qb in range(2 * nb)] + [zero] * nb + [hidden] * nb, axis=-1)
            for kb in range(nb)]
    return jnp.concatenate(rows, axis=-2)


def _diff_attn(qt, k, vt, bias, lam_rows, subln_g, batch, seq, lam_init):
    t = ATTN_TILE
    assert BIAS_SUB >= MAX_DISTANCE and BIAS_SUB % CHUNK == 0 and t % BIAS_SUB == 0 and seq % t == 0
    n = batch * seq
    nq = seq // t
    sched, n_far = _attn_schedule(nq)
    n_steps = sched.shape[1]
    assert ATTN_UNROLL % 2 == 0 and n_far % ATTN_UNROLL == 0 and n_steps % ATTN_UNROLL == 0
    in_specs = [pl.BlockSpec(lam_rows.shape, lambda b, h, s: (0, 0)),
                pl.BlockSpec((2 * HEAD_DIM, seq), lambda b, h, s: (h, b)),
                pl.BlockSpec((seq, 2 * HEAD_DIM), lambda b, h, s: (b, h)),
                pl.BlockSpec((V_ROWS, seq), lambda b, h, s: (h, b)),
                pl.BlockSpec((None, t, 4 * t), lambda b, h, s: (h, 0, 0)),
                pl.BlockSpec(subln_g.shape, lambda b, h, s: (0, 0))]
    out_specs = pl.BlockSpec((seq, V_DIM), lambda b, h, s: (b, h))
    blocks = (3 * _nbytes((seq, 2 * HEAD_DIM), BF16) + _nbytes((V_ROWS, seq), BF16) + _nbytes((t, 4 * t), F32))
    scratch = [pltpu.VMEM((2, 2, t, t), F32),
               pltpu.VMEM((nq, 2, V_ROWS, t), F32),
               pltpu.VMEM((nq, 2, 1, t), F32)]
    temps = sum(_nbytes(s.shape, s.dtype) for s in scratch) + 8 * _nbytes((t, t), F32)
    return pl.pallas_call(
        functools.partial(_attn_kernel, lam_init=lam_init, n_far=n_far, n_steps=n_steps),
        grid_spec=pltpu.PrefetchScalarGridSpec(
            num_scalar_prefetch=1, grid=(batch, N_ATT_HEADS), in_specs=in_specs, out_specs=out_specs,
            scratch_shapes=scratch),
        out_shape=jax.ShapeDtypeStruct((n, ATT_WIDTH), BF16),
        compiler_params=pltpu.CompilerParams(dimension_semantics=("arbitrary",) * 2,
                                             vmem_limit_bytes=_vmem_limit(blocks, temps)),
        name="diff_attn",
    )(jnp.asarray(sched), lam_rows, qt, k, vt, bias, subln_g)


def _merge_kernel(x_ref, g1_ref, attn_ref, sgu_ref, wg_ref, wpa_ref, wps_ref, wo_ref, h_ref):
    x = x_ref[...]
    d = x.shape[1]
    xn = (x * _rms_scale(x) * g1_ref[...]).astype(BF16)
    gate = jax.nn.sigmoid(jnp.dot(xn, wg_ref[...], preferred_element_type=F32))
    y_attn = jnp.dot(attn_ref[...], wpa_ref[...], preferred_element_type=F32)
    y_sgu = jnp.dot(sgu_ref[...], wps_ref[...], preferred_element_type=F32)
    merged = gate[:, :d] * y_attn + gate[:, d:] * y_sgu
    h_ref[...] = x + jnp.dot(merged.astype(BF16), wo_ref[...], preferred_element_type=F32)


def _merge(x2, g1, attn, sgu, wg, wpa, wps, wo):
    n, d = x2.shape
    tm = TOKEN_TILE
    full = lambda a: pl.BlockSpec(a.shape, lambda i: (0,) * a.ndim)
    row = lambda w: pl.BlockSpec((tm, w), lambda i: (i, 0))
    blocks = (2 * _nbytes((tm, d), F32) + 2 * _nbytes((tm, ATT_WIDTH), BF16)
              + sum(_nbytes(a.shape, a.dtype) for a in (g1, wg, wpa, wps, wo)))
    temps = 8 * _nbytes((tm, d), F32)
    return pl.pallas_call(
        _merge_kernel, grid=(n // tm,),
        in_specs=[row(d), full(g1), row(ATT_WIDTH), row(SGU_WIDTH), full(wg), full(wpa), full(wps), full(wo)],
        out_specs=row(d), out_shape=jax.ShapeDtypeStruct((n, d), F32),
        compiler_params=pltpu.CompilerParams(dimension_semantics=("arbitrary",),
                                             vmem_limit_bytes=_vmem_limit(blocks, temps)),
        name="merge",
    )(x2, g1, attn, sgu, wg, wpa, wps, wo)


def _ffn_kernel(h_ref, g2_ref, w1_ref, w2_ref, gf_ref, o_ref, *, final_norm):
    h = h_ref[...]
    hn = (h * _rms_scale(h) * g2_ref[...]).astype(BF16)
    d_ff = w1_ref.shape[1]
    out = h
    for c in range(d_ff // FF_CHUNK):
        lo = c * FF_CHUNK
        z = jnp.dot(hn, w1_ref[:, lo:lo + FF_CHUNK], preferred_element_type=F32)
        ff = jnp.square(jnp.maximum(z, 0.0)).astype(BF16)
        out = out + jnp.dot(ff, w2_ref[lo:lo + FF_CHUNK, :], preferred_element_type=F32)
    if final_norm:
        out = out * _rms_scale(out) * gf_ref[...]
    o_ref[...] = out


def _ffn(h, g2, w1, w2, gf, final_norm):
    n, d = h.shape
    tm = TOKEN_TILE
    full = lambda a: pl.BlockSpec(a.shape, lambda i: (0,) * a.ndim)
    row = pl.BlockSpec((tm, d), lambda i: (i, 0))
    blocks = 2 * _nbytes((tm, d), F32) + sum(_nbytes(a.shape, a.dtype) for a in (g2, w1, w2, gf))
    temps = 4 * _nbytes((tm, d), F32) + 3 * _nbytes((tm, FF_CHUNK), F32)
    return pl.pallas_call(
        functools.partial(_ffn_kernel, final_norm=final_norm), grid=(n // tm,),
        in_specs=[row, full(g2), full(w1), full(w2), full(gf)],
        out_specs=row, out_shape=jax.ShapeDtypeStruct((n, d), F32),
        compiler_params=pltpu.CompilerParams(dimension_semantics=("arbitrary",),
                                             vmem_limit_bytes=_vmem_limit(blocks, temps)),
        name="ffn",
    )(h, g2, w1, w2, gf)


def _layer(h, layer_idx, batch, seq, norm1_g, w_in, lam_rows, subln_g, bias, sgu_norm_g, w_spatial,
           b_spatial, w_proj_attn, w_proj_sgu, w_out, norm2_g, w_ff1, w_ff2, normf_g, final_norm):
    lam_init = 0.8 - 0.6 * math.exp(-0.3 * layer_idx)
    o_q, o_k, o_v, o_uv = QK_WIDTH, 2 * QK_WIDTH, 2 * QK_WIDTH + ATT_WIDTH, 2 * QK_WIDTH + ATT_WIDTH + 2 * SGU_WIDTH
    g1 = norm1_g[None, :]
    wqt = w_in[:, :o_q].T.astype(BF16)
    wk = w_in[:, o_q:o_k].astype(BF16)
    wvt = w_in[:, o_k:o_v].T.astype(BF16)
    wuv = w_in[:, o_v:o_uv].astype(BF16)
    wg = w_in[:, o_uv:].astype(BF16)
    bs_tile = jnp.repeat(b_spatial.T, SGU_GROUP_DIM, axis=1)
    k, qt, vt, sgu = _in_proj(h, g1, wk, wqt, wvt, wuv, sgu_norm_g, w_spatial, bs_tile)
    attn = _diff_attn(qt, k, vt, bias, lam_rows, subln_g[None, :], batch, seq, lam_init)
    h = _merge(h, g1, attn, sgu, wg, w_proj_attn.astype(BF16), w_proj_sgu.astype(BF16), w_out.astype(BF16))
    return _ffn(h, norm2_g[None, :], w_ff1.astype(BF16), w_ff2.astype(BF16), normf_g[None, :], final_norm)


def kernel(x, norm1_g, w_in, lam_q1, lam_k1, lam_q2, lam_k2, subln_g, rel_bias, sgu_norm_g, w_spatial,
           b_spatial, w_proj_attn, w_proj_sgu, w_out, norm2_g, w_ff1, w_ff2, normf_g):
    batch, seq, d = x.shape
    depth = w_in.shape[0]
    bias = _bias_tiles(rel_bias, seq, ATTN_TILE)
    h = x.reshape(batch * seq, d)
    for l in range(depth):
        lam_rows = jnp.stack([lam_q1[l], lam_k1[l], lam_q2[l], lam_k2[l]])
        h = _layer(h, l, batch, seq, norm1_g[l], w_in[l], lam_rows, subln_g[l], bias, sgu_norm_g[l],
                   w_spatial[l], b_spatial[l], w_proj_attn[l], w_proj_sgu[l], w_out[l], norm2_g[l],
                   w_ff1[l], w_ff2[l], normf_g, l == depth - 1)
    return h.reshape(batch, seq, d)
```

```python
import functools
import math

import numpy as np

import jax
import jax.numpy as jnp
from jax import lax
from jax.experimental import pallas as pl
from jax.experimental.pallas import tpu as pltpu

F32 = jnp.float32
BF16 = jnp.bfloat16

CHUNK = 64
N_ATT_HEADS = 4
HEAD_DIM = 64
V_DIM = 2 * HEAD_DIM
QK_WIDTH = N_ATT_HEADS * 2 * HEAD_DIM
ATT_WIDTH = N_ATT_HEADS * V_DIM
N_SGU_GROUPS = 4
SGU_GROUP_DIM = 128
SGU_WIDTH = N_SGU_GROUPS * SGU_GROUP_DIM
SGU_CHUNK = 128
N_BUCKETS = 32
MAX_DISTANCE = 128
NORM_EPS = 1e-6
ATTN_SCALE = HEAD_DIM ** -0.5
LOG2_E = math.log2(math.e)
V_ROWS = V_DIM + 16

V7X_VMEM_BYTES = 64 * 1024 * 1024
V7X_LANES = 128

TOKEN_TILE = 1024
ATTN_TILE = 512
ATTN_UNROLL = 8
BIAS_SUB = 128
FF_CHUNK = 1024

_NT_DIMS = (((1,), (1,)), ((), ()))


def _vmem_limit(block_bytes, temp_bytes, resident_bytes=0):
    return int(min(2 * block_bytes + resident_bytes + temp_bytes, V7X_VMEM_BYTES))


def _resident(a):
    return pl.BlockSpec(a.shape, lambda i: (0,) * a.ndim, pipeline_mode=pl.Buffered(1))


def _nbytes(shape, dtype):
    return math.prod(shape) * jnp.dtype(dtype).itemsize


def _rms_scale(x):
    return lax.rsqrt(jnp.mean(x * x, axis=-1, keepdims=True) + NORM_EPS)


def _in_proj_kernel(x_ref, g1_ref, wk_ref, wqt_ref, wvt_ref, wuv_ref, sgn_ref, ws_ref, bs_ref,
                    k_ref, qt_ref, vt_ref, sgu_ref):
    x = x_ref[...]
    xn = (x * _rms_scale(x) * g1_ref[...]).astype(BF16)
    k_ref[...] = jnp.dot(xn, wk_ref[...], preferred_element_type=F32).astype(BF16)
    qt = lax.dot_general(wqt_ref[...], xn, _NT_DIMS, preferred_element_type=F32)
    qt_ref[...] = (qt * (ATTN_SCALE * LOG2_E)).astype(BF16)
    vt = lax.dot_general(wvt_ref[...], xn, _NT_DIMS, preferred_element_type=F32).astype(BF16)
    ones = jnp.ones((V_ROWS - V_DIM, x.shape[0]), BF16)
    for h in range(N_ATT_HEADS):
        vt_ref[h * V_ROWS:h * V_ROWS + V_DIM, :] = vt[h * V_DIM:(h + 1) * V_DIM]
        vt_ref[h * V_ROWS + V_DIM:(h + 1) * V_ROWS, :] = ones

    zuv = jnp.dot(xn, wuv_ref[...], preferred_element_type=F32)
    guv = 0.5 * zuv * (1.0 + lax.erf(zuv * (2.0 ** -0.5)))
    tm = x.shape[0]
    t_idx = lax.broadcasted_iota(jnp.int32, (SGU_CHUNK, SGU_CHUNK), 0)
    s_idx = lax.broadcasted_iota(jnp.int32, (SGU_CHUNK, SGU_CHUNK), 1)
    for g in range(N_SGU_GROUPS):
        lo = g * SGU_GROUP_DIM
        u = guv[:, lo:lo + SGU_GROUP_DIM]
        v = guv[:, SGU_WIDTH + lo:SGU_WIDTH + lo + SGU_GROUP_DIM]
        vn = (v * _rms_scale(v) * sgn_ref[g:g + 1, :]).astype(BF16)
        w = jnp.where(s_idx <= t_idx, ws_ref[g], 0.0).astype(BF16)
        b = bs_ref[:, lo:lo + SGU_GROUP_DIM]
        for c in range(tm // SGU_CHUNK):
            r0 = c * SGU_CHUNK
            mixed = jnp.dot(w, vn[r0:r0 + SGU_CHUNK, :], preferred_element_type=F32) + b
            sgu_ref[r0:r0 + SGU_CHUNK, lo:lo + SGU_GROUP_DIM] = (u[r0:r0 + SGU_CHUNK, :] * mixed).astype(BF16)


def _in_proj(x2, g1, wk, wqt, wvt, wuv, sgn, ws, bs_tile):
    n, d = x2.shape
    tm = TOKEN_TILE
    params = (g1, wk, wqt, wvt, wuv, sgn, ws, bs_tile)
    in_specs = [pl.BlockSpec((tm, d), lambda i: (i, 0))] + [_resident(a) for a in params]
    out_shape = (jax.ShapeDtypeStruct((n, QK_WIDTH), BF16),
                 jax.ShapeDtypeStruct((QK_WIDTH, n), BF16),
                 jax.ShapeDtypeStruct((N_ATT_HEADS * V_ROWS, n), BF16),
                 jax.ShapeDtypeStruct((n, SGU_WIDTH), BF16))
    out_specs = (pl.BlockSpec((tm, QK_WIDTH), lambda i: (i, 0)),
                 pl.BlockSpec((QK_WIDTH, tm), lambda i: (0, i)),
                 pl.BlockSpec((N_ATT_HEADS * V_ROWS, tm), lambda i: (0, i)),
                 pl.BlockSpec((tm, SGU_WIDTH), lambda i: (i, 0)))
    blocks = (_nbytes((tm, d), F32) + 3 * _nbytes((tm, QK_WIDTH), BF16) + _nbytes((N_ATT_HEADS * V_ROWS, tm), BF16))
    resident = sum(_nbytes(a.shape, a.dtype) for a in params)
    temps = _nbytes((tm, d), F32) + 6 * _nbytes((tm, 2 * SGU_WIDTH), F32)
    return pl.pallas_call(
        _in_proj_kernel, grid=(n // tm,), in_specs=in_specs, out_specs=out_specs, out_shape=out_shape,
        compiler_params=pltpu.CompilerParams(dimension_semantics=("arbitrary",),
                                             vmem_limit_bytes=_vmem_limit(blocks, temps, resident)),
        name="in_proj",
    )(x2, *params)


_KIND_DIAG, _KIND_PREV, _KIND_FAR, _KIND_NONE = 0, 1, 2, 3


def _attn_schedule(nq):
    far = [(i, j, _KIND_FAR) for i in range(nq) for j in range(i - 1)]
    n_far = len(far) - len(far) % ATTN_UNROLL
    near = far[n_far:] + [(i, j, _KIND_DIAG if j == i else _KIND_PREV) for i in range(nq) for j in range(max(i - 1, 0), i + 1)]
    near += [(nq - 1, 0, _KIND_NONE)] * (-len(near) % ATTN_UNROLL)
    seen, steps = set(), []
    for i, j, kind in far[:n_far] + near:
        steps.append((i, j, kind, int(i not in seen)))
        seen.add(i)
    return np.asarray(steps, np.int32).T, n_far


def _attn_kernel(sched_ref, lam_ref, qt_ref, k_ref, vt_ref, bias_ref, g_ref, o_ref, s_ref, acc_ref, m_ref,
                 *, lam_init, n_far, n_steps):
    t = ATTN_TILE
    nq = acc_ref.shape[0]
    row_id = lax.broadcasted_iota(jnp.int32, (2 * HEAD_DIM, t), 0)
    acc_ref[...] = jnp.zeros_like(acc_ref)
    m_ref[...] = jnp.zeros_like(m_ref)

    def scores(n, slot, with_bias):
        i, j, kind = sched_ref[0, n], sched_ref[1, n], sched_ref[2, n]
        qt = qt_ref[:, pl.ds(pl.multiple_of(i * t, t), t)]
        kb = k_ref[pl.ds(pl.multiple_of(j * t, t), t), :]
        zero = jnp.zeros_like(qt)
        block_max = []
        for mp, keep in enumerate((row_id < HEAD_DIM, row_id >= HEAD_DIM)):
            qp = jnp.where(keep, qt, zero)
            s = jnp.dot(kb, qp, preferred_element_type=F32)
            if with_bias:
                s = s + bias_ref[:, pl.ds(pl.multiple_of(kind * t, t), t)]
            s_ref[slot, mp] = s
            block_max.append(jnp.max(s, axis=0, keepdims=True))
        return tuple(block_max)

    def finish(n, slot, block_max):
        i, j, first = sched_ref[0, n], sched_ref[1, n], sched_ref[3, n]
        vb = vt_ref[:, pl.ds(pl.multiple_of(j * t, t), t)]
        for mp in range(2):
            m_old = jnp.where(first == 1, -jnp.inf, m_ref[i, mp])
            m_new = jnp.maximum(m_old, block_max[mp])
            p = jnp.exp2(s_ref[slot, mp] - m_new).astype(BF16)
            acc_ref[i, mp] = (jnp.exp2(m_old - m_new) * acc_ref[i, mp]
                              + jnp.dot(vb, p, preferred_element_type=F32))
            m_ref[i, mp] = m_new

    def phase(lo, hi, with_bias):
        def body(it, block_max):
            for u in range(ATTN_UNROLL):
                n = lo + it * ATTN_UNROLL + u
                next_max = scores(jnp.minimum(n + 1, hi - 1), (u + 1) % 2, with_bias)
                finish(n, u % 2, block_max)
                block_max = next_max
            return block_max
        if hi > lo:
            lax.fori_loop(0, (hi - lo) // ATTN_UNROLL, body, scores(lo, 0, with_bias))

    phase(0, n_far, False)
    phase(n_far, n_steps, True)

    lp = lam_ref[...]
    lam = (jnp.exp(jnp.sum(lp[0:1] * lp[1:2], axis=-1, keepdims=True))
           - jnp.exp(jnp.sum(lp[2:3] * lp[3:4], axis=-1, keepdims=True)) + lam_init)
    gain = g_ref[...]

    for i in range(nq):
        heads = [acc_ref[i, mp, :V_DIM, :] * (1.0 / acc_ref[i, mp, V_DIM:V_DIM + 1, :]) for mp in range(2)]
        o = heads[0] - lam * heads[1]
        o = o * lax.rsqrt(jnp.mean(o * o, axis=0, keepdims=True) + NORM_EPS)
        o_ref[i * t:(i + 1) * t, :] = ((o.T * gain) * (1.0 - lam_init)).astype(BF16)


def _t5_bucket(rel):
    half = N_BUCKETS // 2
    ret = (rel > 0).astype(jnp.int32) * half
    n = jnp.abs(rel)
    max_exact = half // 2
    nf = jnp.maximum(n, 1).astype(F32)
    large = max_exact + (jnp.log(nf / max_exact) / math.log(MAX_DISTANCE / max_exact)
                         * (half - max_exact)).astype(jnp.int32)
    large = jnp.minimum(large, half - 1)
    return ret + jnp.where(n < max_exact, n, large)


def _bias_tiles(rel_bias, seq, t):
    r = BIAS_SUB
    period = 3 * r
    buckets = jnp.arange(N_BUCKETS, dtype=jnp.int32)
    table = rel_bias.astype(F32)
    dist = r - 1 - jnp.arange(period, dtype=jnp.int32)
    onehot = _t5_bucket(dist)[:, None] == buckets[None, :]
    vec = jnp.sum(jnp.where(onehot[:, :, None], table[None], 0.0), axis=1)
    far = _t5_bucket(jnp.asarray(-(seq - 1), jnp.int32)) == buckets
    vec = (vec - jnp.sum(jnp.where(far[:, None], table, 0.0), axis=0)).T
    heads = vec.shape[0]
    band = jnp.tile(vec, (1, r))[:, :r * (period - 1)].reshape(heads, r, period - 1)[:, :, r - 1:3 * r - 1]
    a = jnp.arange(r, dtype=jnp.int32)[:, None]
    c = jnp.arange(r, dtype=jnp.int32)[None, :]
    on_diag = jnp.where(((a // CHUNK) <= (c // CHUNK))[None], band[:, :, :r] * LOG2_E, -jnp.inf)
    below = band[:, :, r:] * LOG2_E
    zero = jnp.zeros_like(below)
    hidden = jnp.full_like(below, -jnp.inf)
    nb = t // r

    def sub_tile(key_blk, query_blk):
        d = key_blk - query_blk
        return hidden if d > 0 else on_diag if d == 0 else below if d == -1 else zero

    rows = [jnp.concatenate([sub_tile(kb, qb) for qb in range(2 * nb)] + [zero] * nb + [hidden] * nb, axis=-1)
            for kb in range(nb)]
    return jnp.concatenate(rows, axis=-2)


def _diff_attn(qt, k, vt, bias, lam_rows, subln_g, batch, seq, lam_init):
    t = ATTN_TILE
    assert BIAS_SUB >= MAX_DISTANCE and BIAS_SUB % CHUNK == 0 and t % BIAS_SUB == 0 and seq % t == 0
    n = batch * seq
    nq = seq // t
    sched, n_far = _attn_schedule(nq)
    n_steps = sched.shape[1]
    assert ATTN_UNROLL % 2 == 0 and n_far % ATTN_UNROLL == 0 and n_steps % ATTN_UNROLL == 0
    in_specs = [pl.BlockSpec(lam_rows.shape, lambda b, h, s: (0, 0)),
                pl.BlockSpec((2 * HEAD_DIM, seq), lambda b, h, s: (h, b)),
                pl.BlockSpec((seq, 2 * HEAD_DIM), lambda b, h, s: (b, h)),
                pl.BlockSpec((V_ROWS, seq), lambda b, h, s: (h, b)),
                pl.BlockSpec((None, t, 4 * t), lambda b, h, s: (h, 0, 0)),
                pl.BlockSpec(subln_g.shape, lambda b, h, s: (0, 0))]
    out_specs = pl.BlockSpec((seq, V_DIM), lambda b, h, s: (b, h))
    blocks = (3 * _nbytes((seq, 2 * HEAD_DIM), BF16) + _nbytes((V_ROWS, seq), BF16) + _nbytes((t, 4 * t), F32))
    scratch = [pltpu.VMEM((2, 2, t, t), F32),
               pltpu.VMEM((nq, 2, V_ROWS, t), F32),
               pltpu.VMEM((nq, 2, 1, t), F32)]
    temps = sum(_nbytes(s.shape, s.dtype) for s in scratch) + 8 * _nbytes((t, t), F32)
    return pl.pallas_call(
        functools.partial(_attn_kernel, lam_init=lam_init, n_far=n_far, n_steps=n_steps),
        grid_spec=pltpu.PrefetchScalarGridSpec(
            num_scalar_prefetch=1, grid=(batch, N_ATT_HEADS), in_specs=in_specs, out_specs=out_specs,
            scratch_shapes=scratch),
        out_shape=jax.ShapeDtypeStruct((n, ATT_WIDTH), BF16),
        compiler_params=pltpu.CompilerParams(dimension_semantics=("arbitrary",) * 2,
                                             vmem_limit_bytes=_vmem_limit(blocks, temps)),
        name="diff_attn",
    )(jnp.asarray(sched), lam_rows, qt, k, vt, bias, subln_g)


def _merge_kernel(x_ref, g1_ref, attn_ref, sgu_ref, wg_ref, wpa_ref, wps_ref, wo_ref, h_ref):
    x = x_ref[...]
    d = x.shape[1]
    xn = (x * _rms_scale(x) * g1_ref[...]).astype(BF16)
    gate = jax.nn.sigmoid(jnp.dot(xn, wg_ref[...], preferred_element_type=F32))
    y_attn = jnp.dot(attn_ref[...], wpa_ref[...], preferred_element_type=F32)
    y_sgu = jnp.dot(sgu_ref[...], wps_ref[...], preferred_element_type=F32)
    merged = gate[:, :d] * y_attn + gate[:, d:] * y_sgu
    h_ref[...] = x + jnp.dot(merged.astype(BF16), wo_ref[...], preferred_element_type=F32)


def _merge(x2, g1, attn, sgu, wg, wpa, wps, wo):
    n, d = x2.shape
    tm = TOKEN_TILE
    row = lambda w: pl.BlockSpec((tm, w), lambda i: (i, 0))
    blocks = 2 * _nbytes((tm, d), F32) + 2 * _nbytes((tm, ATT_WIDTH), BF16)
    resident = sum(_nbytes(a.shape, a.dtype) for a in (g1, wg, wpa, wps, wo))
    temps = 8 * _nbytes((tm, d), F32)
    return pl.pallas_call(
        _merge_kernel, grid=(n // tm,),
        in_specs=[row(d), _resident(g1), row(ATT_WIDTH), row(SGU_WIDTH),
                  _resident(wg), _resident(wpa), _resident(wps), _resident(wo)],
        out_specs=row(d), out_shape=jax.ShapeDtypeStruct((n, d), F32),
        compiler_params=pltpu.CompilerParams(dimension_semantics=("arbitrary",),
                                             vmem_limit_bytes=_vmem_limit(blocks, temps, resident)),
        name="merge",
    )(x2, g1, attn, sgu, wg, wpa, wps, wo)


def _ffn_kernel(h_ref, g2_ref, w1_ref, w2_ref, gf_ref, o_ref, *, final_norm):
    h = h_ref[...]
    hn = (h * _rms_scale(h) * g2_ref[...]).astype(BF16)
    d_ff = w1_ref.shape[1]
    out = h
    for c in range(d_ff // FF_CHUNK):
        lo = c * FF_CHUNK
        z = jnp.dot(hn, w1_ref[:, lo:lo + FF_CHUNK], preferred_element_type=F32)
        ff = jnp.square(jnp.maximum(z, 0.0)).astype(BF16)
        out = out + jnp.dot(ff, w2_ref[lo:lo + FF_CHUNK, :], preferred_element_type=F32)
    if final_norm:
        out = out * _rms_scale(out) * gf_ref[...]
    o_ref[...] = out


def _ffn(h, g2, w1, w2, gf, final_norm):
    n, d = h.shape
    tm = TOKEN_TILE
    row = pl.BlockSpec((tm, d), lambda i: (i, 0))
    blocks = 2 * _nbytes((tm, d), F32)
    resident = sum(_nbytes(a.shape, a.dtype) for a in (g2, w1, w2, gf))
    temps = 4 * _nbytes((tm, d), F32) + 3 * _nbytes((tm, FF_CHUNK), F32)
    return pl.pallas_call(
        functools.partial(_ffn_kernel, final_norm=final_norm), grid=(n // tm,),
        in_specs=[row, _resident(g2), _resident(w1), _resident(w2), _resident(gf)],
        out_specs=row, out_shape=jax.ShapeDtypeStruct((n, d), F32),
        compiler_params=pltpu.CompilerParams(dimension_semantics=("arbitrary",),
                                             vmem_limit_bytes=_vmem_limit(blocks, temps, resident)),
        name="ffn",
    )(h, g2, w1, w2, gf)


def _layer(h, layer_idx, batch, seq, norm1_g, w_in, lam_rows, subln_g, bias, sgu_norm_g, w_spatial,
           b_spatial, w_proj_attn, w_proj_sgu, w_out, norm2_g, w_ff1, w_ff2, normf_g, final_norm):
    lam_init = 0.8 - 0.6 * math.exp(-0.3 * layer_idx)
    o_q, o_k, o_v, o_uv = QK_WIDTH, 2 * QK_WIDTH, 2 * QK_WIDTH + ATT_WIDTH, 2 * QK_WIDTH + ATT_WIDTH + 2 * SGU_WIDTH
    g1 = norm1_g[None, :]
    wqt = w_in[:, :o_q].T.astype(BF16)
    wk = w_in[:, o_q:o_k].astype(BF16)
    wvt = w_in[:, o_k:o_v].T.astype(BF16)
    wuv = w_in[:, o_v:o_uv].astype(BF16)
    wg = w_in[:, o_uv:].astype(BF16)
    bs_tile = jnp.repeat(b_spatial.T, SGU_GROUP_DIM, axis=1)
    k, qt, vt, sgu = _in_proj(h, g1, wk, wqt, wvt, wuv, sgu_norm_g, w_spatial, bs_tile)
    attn = _diff_attn(qt, k, vt, bias, lam_rows, subln_g[None, :], batch, seq, lam_init)
    h = _merge(h, g1, attn, sgu, wg, w_proj_attn.astype(BF16), w_proj_sgu.astype(BF16), w_out.astype(BF16))
    return _ffn(h, norm2_g[None, :], w_ff1.astype(BF16), w_ff2.astype(BF16), normf_g[None, :], final_norm)


def kernel(x, norm1_g, w_in, lam_q1, lam_k1, lam_q2, lam_k2, subln_g, rel_bias, sgu_norm_g, w_spatial,
           b_spatial, w_proj_attn, w_proj_sgu, w_out, norm2_g, w_ff1, w_ff2, normf_g):
    batch, seq, d = x.shape
    depth = w_in.shape[0]
    bias = _bias_tiles(rel_bias, seq, ATTN_TILE)
    h = x.reshape(batch * seq, d)
    for l in range(depth):
        lam_rows = jnp.stack([lam_q1[l], lam_k1[l], lam_q2[l], lam_k2[l]])
        h = _layer(h, l, batch, seq, norm1_g[l], w_in[l], lam_rows, subln_g[l], bias, sgu_norm_g[l],
                   w_spatial[l], b_spatial[l], w_proj_attn[l], w_proj_sgu[l], w_out[l], norm2_g[l],
                   w_ff1[l], w_ff2[l], normf_g, l == depth - 1)
    return h.reshape(batch, seq, d)
```

```python
import functools
import math

import numpy as np

import jax
import jax.numpy as jnp
from jax import lax
from jax.experimental import pallas as pl
from jax.experimental.pallas import tpu as pltpu

F32 = jnp.float32
BF16 = jnp.bfloat16

CHUNK = 64
N_ATT_HEADS = 4
HEAD_DIM = 64
V_DIM = 2 * HEAD_DIM
QK_WIDTH = N_ATT_HEADS * 2 * HEAD_DIM
ATT_WIDTH = N_ATT_HEADS * V_DIM
N_SGU_GROUPS = 4
SGU_GROUP_DIM = 128
SGU_WIDTH = N_SGU_GROUPS * SGU_GROUP_DIM
SGU_CHUNK = 128
N_BUCKETS = 32
MAX_DISTANCE = 128
NORM_EPS = 1e-6
ATTN_SCALE = HEAD_DIM ** -0.5
LOG2_E = math.log2(math.e)
V_ROWS = V_DIM + 16

V7X_VMEM_BYTES = 64 * 1024 * 1024
V7X_LANES = 128

TOKEN_TILE = 1024
ATTN_TILE = 512
ATTN_UNROLL = 8
MERGE_SPLIT = 2
BIAS_SUB = 128
FF_CHUNK = 1024

_NT_DIMS = (((1,), (1,)), ((), ()))


def _vmem_limit(block_bytes, temp_bytes, resident_bytes=0):
    return int(min(2 * block_bytes + resident_bytes + temp_bytes, V7X_VMEM_BYTES))


def _resident(a):
    return pl.BlockSpec(a.shape, lambda i: (0,) * a.ndim, pipeline_mode=pl.Buffered(1))


def _nbytes(shape, dtype):
    return math.prod(shape) * jnp.dtype(dtype).itemsize


def _rms_scale(x):
    return lax.rsqrt(jnp.mean(x * x, axis=-1, keepdims=True) + NORM_EPS)


def _in_proj_kernel(x_ref, g1_ref, wk_ref, wqt_ref, wvt_ref, wuv_ref, sgn_ref, ws_ref, bs_ref,
                    k_ref, qt_ref, vt_ref, sgu_ref):
    x = x_ref[...]
    xn = (x * _rms_scale(x) * g1_ref[...]).astype(BF16)
    zuv = jnp.dot(xn, wuv_ref[...], preferred_element_type=F32)
    k_ref[...] = jnp.dot(xn, wk_ref[...], preferred_element_type=F32).astype(BF16)
    qt = lax.dot_general(wqt_ref[...], xn, _NT_DIMS, preferred_element_type=F32)
    qt_ref[...] = (qt * (ATTN_SCALE * LOG2_E)).astype(BF16)
    vt = lax.dot_general(wvt_ref[...], xn, _NT_DIMS, preferred_element_type=F32).astype(BF16)
    ones = jnp.ones((V_ROWS - V_DIM, x.shape[0]), BF16)
    for h in range(N_ATT_HEADS):
        vt_ref[h * V_ROWS:h * V_ROWS + V_DIM, :] = vt[h * V_DIM:(h + 1) * V_DIM]
        vt_ref[h * V_ROWS + V_DIM:(h + 1) * V_ROWS, :] = ones

    guv = 0.5 * zuv * (1.0 + lax.erf(zuv * (2.0 ** -0.5)))
    n_chunks = x.shape[0] // SGU_CHUNK
    t_idx = lax.broadcasted_iota(jnp.int32, (SGU_CHUNK, SGU_CHUNK), 0)
    s_idx = lax.broadcasted_iota(jnp.int32, (SGU_CHUNK, SGU_CHUNK), 1)
    for g in range(N_SGU_GROUPS):
        lo = g * SGU_GROUP_DIM
        u = guv[:, lo:lo + SGU_GROUP_DIM]
        v = guv[:, SGU_WIDTH + lo:SGU_WIDTH + lo + SGU_GROUP_DIM]
        vn = (v * _rms_scale(v) * sgn_ref[g:g + 1, :]).astype(BF16)
        w = jnp.where(s_idx <= t_idx, ws_ref[g], 0.0).astype(BF16)
        b = bs_ref[:, lo:lo + SGU_GROUP_DIM]
        chunks = jnp.concatenate([vn[c * SGU_CHUNK:(c + 1) * SGU_CHUNK, :] for c in range(n_chunks)], axis=1)
        mixed = jnp.dot(w, chunks, preferred_element_type=F32)
        for c in range(n_chunks):
            r0 = c * SGU_CHUNK
            gated = u[r0:r0 + SGU_CHUNK, :] * (mixed[:, c * SGU_GROUP_DIM:(c + 1) * SGU_GROUP_DIM] + b)
            sgu_ref[r0:r0 + SGU_CHUNK, lo:lo + SGU_GROUP_DIM] = gated.astype(BF16)


def _in_proj(x2, g1, wk, wqt, wvt, wuv, sgn, ws, bs_tile):
    n, d = x2.shape
    tm = TOKEN_TILE
    params = (g1, wk, wqt, wvt, wuv, sgn, ws, bs_tile)
    in_specs = [pl.BlockSpec((tm, d), lambda i: (i, 0))] + [_resident(a) for a in params]
    out_shape = (jax.ShapeDtypeStruct((n, QK_WIDTH), BF16),
                 jax.ShapeDtypeStruct((QK_WIDTH, n), BF16),
                 jax.ShapeDtypeStruct((N_ATT_HEADS * V_ROWS, n), BF16),
                 jax.ShapeDtypeStruct((n, SGU_WIDTH), BF16))
    out_specs = (pl.BlockSpec((tm, QK_WIDTH), lambda i: (i, 0)),
                 pl.BlockSpec((QK_WIDTH, tm), lambda i: (0, i)),
                 pl.BlockSpec((N_ATT_HEADS * V_ROWS, tm), lambda i: (0, i)),
                 pl.BlockSpec((tm, SGU_WIDTH), lambda i: (i, 0)))
    blocks = (_nbytes((tm, d), F32) + 3 * _nbytes((tm, QK_WIDTH), BF16) + _nbytes((N_ATT_HEADS * V_ROWS, tm), BF16))
    resident = sum(_nbytes(a.shape, a.dtype) for a in params)
    temps = _nbytes((tm, d), F32) + 6 * _nbytes((tm, 2 * SGU_WIDTH), F32)
    return pl.pallas_call(
        _in_proj_kernel, grid=(n // tm,), in_specs=in_specs, out_specs=out_specs, out_shape=out_shape,
        compiler_params=pltpu.CompilerParams(dimension_semantics=("arbitrary",),
                                             vmem_limit_bytes=_vmem_limit(blocks, temps, resident)),
        name="in_proj",
    )(x2, *params)


_KIND_DIAG, _KIND_PREV, _KIND_FAR, _KIND_NONE = 0, 1, 2, 3


def _attn_schedule(nq):
    far = [(i, j, _KIND_FAR) for i in range(nq) for j in range(i - 1)]
    n_far = len(far) - len(far) % ATTN_UNROLL
    near = far[n_far:] + [(i, j, _KIND_DIAG if j == i else _KIND_PREV) for i in range(nq) for j in range(max(i - 1, 0), i + 1)]
    near += [(nq - 1, 0, _KIND_NONE)] * (-len(near) % ATTN_UNROLL)
    seen, steps = set(), []
    for i, j, kind in far[:n_far] + near:
        steps.append((i, j, kind, int(i not in seen)))
        seen.add(i)
    return np.asarray(steps, np.int32).T, n_far


def _attn_kernel(sched_ref, lam_ref, qt_ref, k_ref, vt_ref, bias_ref, g_ref, o_ref, s_ref, acc_ref, m_ref,
                 *, lam_init, n_far, n_steps):
    t = ATTN_TILE
    nq = acc_ref.shape[0]
    row_id = lax.broadcasted_iota(jnp.int32, (2 * HEAD_DIM, t), 0)
    acc_ref[...] = jnp.zeros_like(acc_ref)
    m_ref[...] = jnp.zeros_like(m_ref)

    def scores(n, slot, with_bias):
        i, j, kind = sched_ref[0, n], sched_ref[1, n], sched_ref[2, n]
        qt = qt_ref[:, pl.ds(pl.multiple_of(i * t, t), t)]
        kb = k_ref[pl.ds(pl.multiple_of(j * t, t), t), :]
        zero = jnp.zeros_like(qt)
        block_max = []
        for mp, keep in enumerate((row_id < HEAD_DIM, row_id >= HEAD_DIM)):
            qp = jnp.where(keep, qt, zero)
            s = jnp.dot(kb, qp, preferred_element_type=F32)
            if with_bias:
                s = s + bias_ref[:, pl.ds(pl.multiple_of(kind * t, t), t)]
            s_ref[slot, mp] = s
            block_max.append(jnp.max(s, axis=0, keepdims=True))
        return tuple(block_max)

    def finish(n, slot, block_max):
        i, j, first = sched_ref[0, n], sched_ref[1, n], sched_ref[3, n]
        vb = vt_ref[:, pl.ds(pl.multiple_of(j * t, t), t)]
        for mp in range(2):
            m_old = jnp.where(first == 1, -jnp.inf, m_ref[i, mp])
            m_new = jnp.maximum(m_old, block_max[mp])
            p = jnp.exp2(s_ref[slot, mp] - m_new).astype(BF16)
            acc_ref[i, mp] = (jnp.exp2(m_old - m_new) * acc_ref[i, mp]
                              + jnp.dot(vb, p, preferred_element_type=F32))
            m_ref[i, mp] = m_new

    def phase(lo, hi, with_bias):
        def body(it, block_max):
            for u in range(ATTN_UNROLL):
                n = lo + it * ATTN_UNROLL + u
                next_max = scores(jnp.minimum(n + 1, hi - 1), (u + 1) % 2, with_bias)
                finish(n, u % 2, block_max)
                block_max = next_max
            return block_max
        if hi > lo:
            lax.fori_loop(0, (hi - lo) // ATTN_UNROLL, body, scores(lo, 0, with_bias))

    phase(0, n_far, False)
    phase(n_far, n_steps, True)

    lp = lam_ref[...]
    lam = (jnp.exp(jnp.sum(lp[0:1] * lp[1:2], axis=-1, keepdims=True))
           - jnp.exp(jnp.sum(lp[2:3] * lp[3:4], axis=-1, keepdims=True)) + lam_init)
    gain = g_ref[...]

    for i in range(nq):
        heads = [acc_ref[i, mp, :V_DIM, :] * (1.0 / acc_ref[i, mp, V_DIM:V_DIM + 1, :]) for mp in range(2)]
        o = heads[0] - lam * heads[1]
        o = o * lax.rsqrt(jnp.mean(o * o, axis=0, keepdims=True) + NORM_EPS)
        o_ref[i * t:(i + 1) * t, :] = ((o.T * gain) * (1.0 - lam_init)).astype(BF16)


def _t5_bucket(rel):
    half = N_BUCKETS // 2
    ret = (rel > 0).astype(jnp.int32) * half
    n = jnp.abs(rel)
    max_exact = half // 2
    nf = jnp.maximum(n, 1).astype(F32)
    large = max_exact + (jnp.log(nf / max_exact) / math.log(MAX_DISTANCE / max_exact)
                         * (half - max_exact)).astype(jnp.int32)
    large = jnp.minimum(large, half - 1)
    return ret + jnp.where(n < max_exact, n, large)


def _bias_tiles(rel_bias, seq, t):
    r = BIAS_SUB
    period = 3 * r
    buckets = jnp.arange(N_BUCKETS, dtype=jnp.int32)
    table = rel_bias.astype(F32)
    dist = r - 1 - jnp.arange(period, dtype=jnp.int32)
    onehot = _t5_bucket(dist)[:, None] == buckets[None, :]
    vec = jnp.sum(jnp.where(onehot[:, :, None], table[None], 0.0), axis=1)
    far = _t5_bucket(jnp.asarray(-(seq - 1), jnp.int32)) == buckets
    vec = (vec - jnp.sum(jnp.where(far[:, None], table, 0.0), axis=0)).T
    heads = vec.shape[0]
    band = jnp.tile(vec, (1, r))[:, :r * (period - 1)].reshape(heads, r, period - 1)[:, :, r - 1:3 * r - 1]
    a = jnp.arange(r, dtype=jnp.int32)[:, None]
    c = jnp.arange(r, dtype=jnp.int32)[None, :]
    on_diag = jnp.where(((a // CHUNK) <= (c // CHUNK))[None], band[:, :, :r] * LOG2_E, -jnp.inf)
    below = band[:, :, r:] * LOG2_E
    zero = jnp.zeros_like(below)
    hidden = jnp.full_like(below, -jnp.inf)
    nb = t // r

    def sub_tile(key_blk, query_blk):
        d = key_blk - query_blk
        return hidden if d > 0 else on_diag if d == 0 else below if d == -1 else zero

    rows = [jnp.concatenate([sub_tile(kb, qb) for qb in range(2 * nb)] + [zero] * nb + [hidden] * nb, axis=-1)
            for kb in range(nb)]
    return jnp.concatenate(rows, axis=-2)


def _diff_attn(qt, k, vt, bias, lam_rows, subln_g, batch, seq, lam_init):
    t = ATTN_TILE
    assert BIAS_SUB >= MAX_DISTANCE and BIAS_SUB % CHUNK == 0 and t % BIAS_SUB == 0 and seq % t == 0
    n = batch * seq
    nq = seq // t
    sched, n_far = _attn_schedule(nq)
    n_steps = sched.shape[1]
    assert ATTN_UNROLL % 2 == 0 and n_far % ATTN_UNROLL == 0 and n_steps % ATTN_UNROLL == 0
    in_specs = [pl.BlockSpec(lam_rows.shape, lambda b, h, s: (0, 0)),
                pl.BlockSpec((2 * HEAD_DIM, seq), lambda b, h, s: (h, b)),
                pl.BlockSpec((seq, 2 * HEAD_DIM), lambda b, h, s: (b, h)),
                pl.BlockSpec((V_ROWS, seq), lambda b, h, s: (h, b)),
                pl.BlockSpec((None, t, 4 * t), lambda b, h, s: (h, 0, 0)),
                pl.BlockSpec(subln_g.shape, lambda b, h, s: (0, 0))]
    out_specs = pl.BlockSpec((seq, V_DIM), lambda b, h, s: (b, h))
    blocks = (3 * _nbytes((seq, 2 * HEAD_DIM), BF16) + _nbytes((V_ROWS, seq), BF16) + _nbytes((t, 4 * t), F32))
    scratch = [pltpu.VMEM((2, 2, t, t), F32),
               pltpu.VMEM((nq, 2, V_ROWS, t), F32),
               pltpu.VMEM((nq, 2, 1, t), F32)]
    temps = sum(_nbytes(s.shape, s.dtype) for s in scratch) + 8 * _nbytes((t, t), F32)
    return pl.pallas_call(
        functools.partial(_attn_kernel, lam_init=lam_init, n_far=n_far, n_steps=n_steps),
        grid_spec=pltpu.PrefetchScalarGridSpec(
            num_scalar_prefetch=1, grid=(batch, N_ATT_HEADS), in_specs=in_specs, out_specs=out_specs,
            scratch_shapes=scratch),
        out_shape=jax.ShapeDtypeStruct((n, ATT_WIDTH), BF16),
        compiler_params=pltpu.CompilerParams(dimension_semantics=("arbitrary",) * 2,
                                             vmem_limit_bytes=_vmem_limit(blocks, temps)),
        name="diff_attn",
    )(jnp.asarray(sched), lam_rows, qt, k, vt, bias, subln_g)


def _merge_kernel(x_ref, g1_ref, attn_ref, sgu_ref, wg_ref, wpa_ref, wps_ref, wo_ref, h_ref):
    tm, d = x_ref.shape
    rows = tm // MERGE_SPLIT
    parts = []
    for a in range(MERGE_SPLIT):
        sl = slice(a * rows, (a + 1) * rows)
        x = x_ref[sl, :]
        xn = (x * _rms_scale(x) * g1_ref[...]).astype(BF16)
        gate = jax.nn.sigmoid(jnp.dot(xn, wg_ref[...], preferred_element_type=F32))
        y_attn = jnp.dot(attn_ref[sl, :], wpa_ref[...], preferred_element_type=F32)
        y_sgu = jnp.dot(sgu_ref[sl, :], wps_ref[...], preferred_element_type=F32)
        parts.append((sl, x, (gate[:, :d] * y_attn + gate[:, d:] * y_sgu).astype(BF16)))
    for sl, x, merged in parts:
        h_ref[sl, :] = x + jnp.dot(merged, wo_ref[...], preferred_element_type=F32)


def _merge(x2, g1, attn, sgu, wg, wpa, wps, wo):
    n, d = x2.shape
    tm = TOKEN_TILE
    row = lambda w: pl.BlockSpec((tm, w), lambda i: (i, 0))
    blocks = 2 * _nbytes((tm, d), F32) + 2 * _nbytes((tm, ATT_WIDTH), BF16)
    resident = sum(_nbytes(a.shape, a.dtype) for a in (g1, wg, wpa, wps, wo))
    temps = 8 * _nbytes((tm, d), F32)
    return pl.pallas_call(
        _merge_kernel, grid=(n // tm,),
        in_specs=[row(d), _resident(g1), row(ATT_WIDTH), row(SGU_WIDTH),
                  _resident(wg), _resident(wpa), _resident(wps), _resident(wo)],
        out_specs=row(d), out_shape=jax.ShapeDtypeStruct((n, d), F32),
        compiler_params=pltpu.CompilerParams(dimension_semantics=("arbitrary",),
                                             vmem_limit_bytes=_vmem_limit(blocks, temps, resident)),
        name="merge",
    )(x2, g1, attn, sgu, wg, wpa, wps, wo)


def _ffn_kernel(h_ref, g2_ref, w1_ref, w2_ref, gf_ref, o_ref, *, final_norm):
    h = h_ref[...]
    hn = (h * _rms_scale(h) * g2_ref[...]).astype(BF16)
    n_chunks = w1_ref.shape[1] // FF_CHUNK
    up = lambda c: jnp.dot(hn, w1_ref[:, c * FF_CHUNK:(c + 1) * FF_CHUNK], preferred_element_type=F32)
    out = h
    z = up(0)
    for c in range(n_chunks):
        z_next = up(c + 1) if c + 1 < n_chunks else None
        ff = jnp.square(jnp.maximum(z, 0.0)).astype(BF16)
        out = out + jnp.dot(ff, w2_ref[c * FF_CHUNK:(c + 1) * FF_CHUNK, :], preferred_element_type=F32)
        z = z_next
    if final_norm:
        out = out * _rms_scale(out) * gf_ref[...]
    o_ref[...] = out


def _ffn(h, g2, w1, w2, gf, final_norm):
    n, d = h.shape
    tm = TOKEN_TILE
    row = pl.BlockSpec((tm, d), lambda i: (i, 0))
    blocks = 2 * _nbytes((tm, d), F32)
    resident = sum(_nbytes(a.shape, a.dtype) for a in (g2, w1, w2, gf))
    temps = 4 * _nbytes((tm, d), F32) + 3 * _nbytes((tm, FF_CHUNK), F32)
    return pl.pallas_call(
        functools.partial(_ffn_kernel, final_norm=final_norm), grid=(n // tm,),
        in_specs=[row, _resident(g2), _resident(w1), _resident(w2), _resident(gf)],
        out_specs=row, out_shape=jax.ShapeDtypeStruct((n, d), F32),
        compiler_params=pltpu.CompilerParams(dimension_semantics=("arbitrary",),
                                             vmem_limit_bytes=_vmem_limit(blocks, temps, resident)),
        name="ffn",
    )(h, g2, w1, w2, gf)


def _layer(h, layer_idx, batch, seq, norm1_g, w_in, lam_rows, subln_g, bias, sgu_norm_g, w_spatial,
           b_spatial, w_proj_attn, w_proj_sgu, w_out, norm2_g, w_ff1, w_ff2, normf_g, final_norm):
    lam_init = 0.8 - 0.6 * math.exp(-0.3 * layer_idx)
    o_q, o_k, o_v, o_uv = QK_WIDTH, 2 * QK_WIDTH, 2 * QK_WIDTH + ATT_WIDTH, 2 * QK_WIDTH + ATT_WIDTH + 2 * SGU_WIDTH
    g1 = norm1_g[None, :]
    wqt = w_in[:, :o_q].T.astype(BF16)
    wk = w_in[:, o_q:o_k].astype(BF16)
    wvt = w_in[:, o_k:o_v].T.astype(BF16)
    wuv = w_in[:, o_v:o_uv].astype(BF16)
    wg = w_in[:, o_uv:].astype(BF16)
    bs_tile = jnp.repeat(b_spatial.T, SGU_GROUP_DIM, axis=1)
    k, qt, vt, sgu = _in_proj(h, g1, wk, wqt, wvt, wuv, sgu_norm_g, w_spatial, bs_tile)
    attn = _diff_attn(qt, k, vt, bias, lam_rows, subln_g[None, :], batch, seq, lam_init)
    h = _merge(h, g1, attn, sgu, wg, w_proj_attn.astype(BF16), w_proj_sgu.astype(BF16), w_out.astype(BF16))
    return _ffn(h, norm2_g[None, :], w_ff1.astype(BF16), w_ff2.astype(BF16), normf_g[None, :], final_norm)


def kernel(x, norm1_g, w_in, lam_q1, lam_k1, lam_q2, lam_k2, subln_g, rel_bias, sgu_norm_g, w_spatial,
           b_spatial, w_proj_attn, w_proj_sgu, w_out, norm2_g, w_ff1, w_ff2, normf_g):
    batch, seq, d = x.shape
    depth = w_in.shape[0]
    bias = _bias_tiles(rel_bias, seq, ATTN_TILE)
    h = x.reshape(batch * seq, d)
    for l in range(depth):
        lam_rows = jnp.stack([lam_q1[l], lam_k1[l], lam_q2[l], lam_k2[l]])
        h = _layer(h, l, batch, seq, norm1_g[l], w_in[l], lam_rows, subln_g[l], bias, sgu_norm_g[l],
                   w_spatial[l], b_spatial[l], w_proj_attn[l], w_proj_sgu[l], w_out[l], norm2_g[l],
                   w_ff1[l], w_ff2[l], normf_g, l == depth - 1)
    return h.reshape(batch, seq, d)
```

```python
import functools
import math

import numpy as np

import jax
import jax.numpy as jnp
from jax import lax
from jax.experimental import pallas as pl
from jax.experimental.pallas import tpu as pltpu

F32 = jnp.float32
BF16 = jnp.bfloat16

CHUNK = 64
N_ATT_HEADS = 4
HEAD_DIM = 64
V_DIM = 2 * HEAD_DIM
QK_WIDTH = N_ATT_HEADS * 2 * HEAD_DIM
ATT_WIDTH = N_ATT_HEADS * V_DIM
N_SGU_GROUPS = 4
SGU_GROUP_DIM = 128
SGU_WIDTH = N_SGU_GROUPS * SGU_GROUP_DIM
SGU_CHUNK = 128
N_BUCKETS = 32
MAX_DISTANCE = 128
NORM_EPS = 1e-6
ATTN_SCALE = HEAD_DIM ** -0.5
LOG2_E = math.log2(math.e)
V_ROWS = V_DIM + 16

V7X_VMEM_BYTES = 64 * 1024 * 1024
V7X_LANES = 128

TOKEN_TILE = 1024
ATTN_TILE = 512
ATTN_UNROLL = 8
ATTN_UNROLL_FAR = 26
MERGE_SPLIT = 2
BIAS_SUB = 128
FF_CHUNK = 1024

_NT_DIMS = (((1,), (1,)), ((), ()))


def _vmem_limit(block_bytes, temp_bytes, resident_bytes=0):
    return int(min(2 * block_bytes + resident_bytes + temp_bytes, V7X_VMEM_BYTES))


def _resident(a):
    return pl.BlockSpec(a.shape, lambda i: (0,) * a.ndim, pipeline_mode=pl.Buffered(1))


def _nbytes(shape, dtype):
    return math.prod(shape) * jnp.dtype(dtype).itemsize


def _rms_scale(x):
    return lax.rsqrt(jnp.mean(x * x, axis=-1, keepdims=True) + NORM_EPS)


def _in_proj_kernel(x_ref, g1_ref, wk_ref, wqt_ref, wvt_ref, wuv_ref, sgn_ref, ws_ref, bs_ref,
                    k_ref, qt_ref, vt_ref, sgu_ref):
    x = x_ref[...]
    xn = (x * _rms_scale(x) * g1_ref[...]).astype(BF16)
    zuv = jnp.dot(xn, wuv_ref[...], preferred_element_type=F32)
    k_ref[...] = jnp.dot(xn, wk_ref[...], preferred_element_type=F32).astype(BF16)
    qt = lax.dot_general(wqt_ref[...], xn, _NT_DIMS, preferred_element_type=F32)
    qt_ref[...] = (qt * (ATTN_SCALE * LOG2_E)).astype(BF16)
    vt = lax.dot_general(wvt_ref[...], xn, _NT_DIMS, preferred_element_type=F32).astype(BF16)
    ones = jnp.ones((V_ROWS - V_DIM, x.shape[0]), BF16)
    for h in range(N_ATT_HEADS):
        vt_ref[h * V_ROWS:h * V_ROWS + V_DIM, :] = vt[h * V_DIM:(h + 1) * V_DIM]
        vt_ref[h * V_ROWS + V_DIM:(h + 1) * V_ROWS, :] = ones

    guv = 0.5 * zuv * (1.0 + lax.erf(zuv * (2.0 ** -0.5)))
    n_chunks = x.shape[0] // SGU_CHUNK
    t_idx = lax.broadcasted_iota(jnp.int32, (SGU_CHUNK, SGU_CHUNK), 0)
    s_idx = lax.broadcasted_iota(jnp.int32, (SGU_CHUNK, SGU_CHUNK), 1)
    for g in range(N_SGU_GROUPS):
        lo = g * SGU_GROUP_DIM
        u = guv[:, lo:lo + SGU_GROUP_DIM]
        v = guv[:, SGU_WIDTH + lo:SGU_WIDTH + lo + SGU_GROUP_DIM]
        vn = (v * _rms_scale(v) * sgn_ref[g:g + 1, :]).astype(BF16)
        w = jnp.where(s_idx <= t_idx, ws_ref[g], 0.0).astype(BF16)
        b = bs_ref[:, lo:lo + SGU_GROUP_DIM]
        chunks = jnp.concatenate([vn[c * SGU_CHUNK:(c + 1) * SGU_CHUNK, :] for c in range(n_chunks)], axis=1)
        mixed = jnp.dot(w, chunks, preferred_element_type=F32)
        for c in range(n_chunks):
            r0 = c * SGU_CHUNK
            gated = u[r0:r0 + SGU_CHUNK, :] * (mixed[:, c * SGU_GROUP_DIM:(c + 1) * SGU_GROUP_DIM] + b)
            sgu_ref[r0:r0 + SGU_CHUNK, lo:lo + SGU_GROUP_DIM] = gated.astype(BF16)


def _in_proj(x2, g1, wk, wqt, wvt, wuv, sgn, ws, bs_tile):
    n, d = x2.shape
    tm = TOKEN_TILE
    params = (g1, wk, wqt, wvt, wuv, sgn, ws, bs_tile)
    in_specs = [pl.BlockSpec((tm, d), lambda i: (i, 0))] + [_resident(a) for a in params]
    out_shape = (jax.ShapeDtypeStruct((n, QK_WIDTH), BF16),
                 jax.ShapeDtypeStruct((QK_WIDTH, n), BF16),
                 jax.ShapeDtypeStruct((N_ATT_HEADS * V_ROWS, n), BF16),
                 jax.ShapeDtypeStruct((n, SGU_WIDTH), BF16))
    out_specs = (pl.BlockSpec((tm, QK_WIDTH), lambda i: (i, 0)),
                 pl.BlockSpec((QK_WIDTH, tm), lambda i: (0, i)),
                 pl.BlockSpec((N_ATT_HEADS * V_ROWS, tm), lambda i: (0, i)),
                 pl.BlockSpec((tm, SGU_WIDTH), lambda i: (i, 0)))
    blocks = (_nbytes((tm, d), F32) + 3 * _nbytes((tm, QK_WIDTH), BF16) + _nbytes((N_ATT_HEADS * V_ROWS, tm), BF16))
    resident = sum(_nbytes(a.shape, a.dtype) for a in params)
    temps = _nbytes((tm, d), F32) + 6 * _nbytes((tm, 2 * SGU_WIDTH), F32)
    return pl.pallas_call(
        _in_proj_kernel, grid=(n // tm,), in_specs=in_specs, out_specs=out_specs, out_shape=out_shape,
        compiler_params=pltpu.CompilerParams(dimension_semantics=("arbitrary",),
                                             vmem_limit_bytes=_vmem_limit(blocks, temps, resident)),
        name="in_proj",
    )(x2, *params)


_KIND_DIAG, _KIND_PREV, _KIND_FAR, _KIND_NONE = 0, 1, 2, 3


def _attn_schedule(nq):
    far = [(i, j, _KIND_FAR) for i in range(nq) for j in range(i - 1)]
    unroll_far = ATTN_UNROLL_FAR if len(far) >= ATTN_UNROLL_FAR else ATTN_UNROLL
    n_far = len(far) - len(far) % unroll_far
    near = far[n_far:] + [(i, j, _KIND_DIAG if j == i else _KIND_PREV) for i in range(nq) for j in range(max(i - 1, 0), i + 1)]
    near += [(nq - 1, 0, _KIND_NONE)] * (-len(near) % ATTN_UNROLL)
    seen, steps = set(), []
    for i, j, kind in far[:n_far] + near:
        steps.append((i, j, kind, int(i not in seen)))
        seen.add(i)
    return np.asarray(steps, np.int32).T, n_far, unroll_far


def _attn_kernel(sched_ref, lam_ref, qt_ref, k_ref, vt_ref, bias_ref, g_ref, o_ref, s_ref, acc_ref, m_ref,
                 *, lam_init, n_far, n_steps, unroll_far, chain):
    t = ATTN_TILE
    row_id = lax.broadcasted_iota(jnp.int32, (2 * HEAD_DIM, t), 0)
    acc_ref[...] = jnp.zeros_like(acc_ref)
    m_ref[...] = jnp.zeros_like(m_ref)
    lp = lam_ref[...]
    lam = (jnp.exp(jnp.sum(lp[0:1] * lp[1:2], axis=-1, keepdims=True))
           - jnp.exp(jnp.sum(lp[2:3] * lp[3:4], axis=-1, keepdims=True)) + lam_init)
    gain = g_ref[...] * (1.0 - lam_init)

    def scores(n, slot, with_bias):
        i, j, kind = sched_ref[0, n], sched_ref[1, n], sched_ref[2, n]
        qt = qt_ref[:, pl.ds(pl.multiple_of(i * t, t), t)]
        kb = k_ref[pl.ds(pl.multiple_of(j * t, t), t), :]
        zero = jnp.zeros_like(qt)
        block_max = []
        for mp, keep in enumerate((row_id < HEAD_DIM, row_id >= HEAD_DIM)):
            qp = jnp.where(keep, qt, zero)
            s = jnp.dot(kb, qp, preferred_element_type=F32)
            if with_bias:
                s = s + bias_ref[:, pl.ds(pl.multiple_of(kind * t, t), t)]
            s_ref[slot, mp] = s
            block_max.append(jnp.max(s, axis=0, keepdims=True))
        return tuple(block_max)

    def finish(n, slot, block_max):
        i, j, first = sched_ref[0, n], sched_ref[1, n], sched_ref[3, n]
        vb = vt_ref[:, pl.ds(pl.multiple_of(j * t, t), t)]
        for mp in range(2):
            m_old = jnp.where(first == 1, -jnp.inf, m_ref[i, mp])
            m_new = jnp.maximum(m_old, block_max[mp])
            p = jnp.exp2(s_ref[slot, mp] - m_new).astype(BF16)
            acc_ref[i, mp] = (jnp.exp2(m_old - m_new) * acc_ref[i, mp]
                              + jnp.dot(vb, p, preferred_element_type=F32))
            m_ref[i, mp] = m_new

    def phase(lo, hi, last, with_bias, unroll, first_max):
        def body(it, block_max):
            for u in range(unroll):
                n = lo + it * unroll + u
                next_max = scores(jnp.minimum(n + 1, last), (u + 1) % 2, with_bias)
                finish(n, u % 2, block_max)
                block_max = next_max
            return block_max
        return lax.fori_loop(0, (hi - lo) // unroll, body, first_max)

    if n_far and chain:
        block_max = phase(0, n_far, n_far, False, unroll_far, scores(0, 0, False))
    else:
        if n_far:
            phase(0, n_far, n_far - 1, False, unroll_far, scores(0, 0, False))
        block_max = scores(n_far, 0, True)
    phase(n_far, n_steps, n_steps - 1, True, ATTN_UNROLL, block_max)

    for i in range(acc_ref.shape[0]):
        heads = [acc_ref[i, mp, :V_DIM, :] * (1.0 / acc_ref[i, mp, V_DIM:V_DIM + 1, :]) for mp in range(2)]
        o = heads[0] - lam * heads[1]
        o = o * lax.rsqrt(jnp.mean(o * o, axis=0, keepdims=True) + NORM_EPS)
        o_ref[i * t:(i + 1) * t, :] = (o.T * gain).astype(BF16)


def _t5_bucket(rel):
    half = N_BUCKETS // 2
    ret = (rel > 0).astype(jnp.int32) * half
    n = jnp.abs(rel)
    max_exact = half // 2
    nf = jnp.maximum(n, 1).astype(F32)
    large = max_exact + (jnp.log(nf / max_exact) / math.log(MAX_DISTANCE / max_exact)
                         * (half - max_exact)).astype(jnp.int32)
    large = jnp.minimum(large, half - 1)
    return ret + jnp.where(n < max_exact, n, large)


def _bias_tiles(rel_bias, seq, t):
    r = BIAS_SUB
    period = 3 * r
    buckets = jnp.arange(N_BUCKETS, dtype=jnp.int32)
    table = rel_bias.astype(F32)
    dist = r - 1 - jnp.arange(period, dtype=jnp.int32)
    onehot = _t5_bucket(dist)[:, None] == buckets[None, :]
    vec = jnp.sum(jnp.where(onehot[:, :, None], table[None], 0.0), axis=1)
    far = _t5_bucket(jnp.asarray(-(seq - 1), jnp.int32)) == buckets
    vec = (vec - jnp.sum(jnp.where(far[:, None], table, 0.0), axis=0)).T
    heads = vec.shape[0]
    band = jnp.tile(vec, (1, r))[:, :r * (period - 1)].reshape(heads, r, period - 1)[:, :, r - 1:3 * r - 1]
    a = jnp.arange(r, dtype=jnp.int32)[:, None]
    c = jnp.arange(r, dtype=jnp.int32)[None, :]
    on_diag = jnp.where(((a // CHUNK) <= (c // CHUNK))[None], band[:, :, :r] * LOG2_E, -jnp.inf)
    below = band[:, :, r:] * LOG2_E
    zero = jnp.zeros_like(below)
    hidden = jnp.full_like(below, -jnp.inf)
    nb = t // r

    def sub_tile(key_blk, query_blk):
        d = key_blk - query_blk
        return hidden if d > 0 else on_diag if d == 0 else below if d == -1 else zero

    rows = [jnp.concatenate([sub_tile(kb, qb) for qb in range(2 * nb)] + [zero] * nb + [hidden] * nb, axis=-1)
            for kb in range(nb)]
    return jnp.concatenate(rows, axis=-2)


def _diff_attn(qt, k, vt, bias, lam_rows, subln_g, batch, seq, lam_init):
    t = ATTN_TILE
    assert BIAS_SUB >= MAX_DISTANCE and BIAS_SUB % CHUNK == 0 and t % BIAS_SUB == 0 and seq % t == 0
    n = batch * seq
    nq = seq // t
    sched, n_far, unroll_far = _attn_schedule(nq)
    n_steps = sched.shape[1]
    assert ATTN_UNROLL % 2 == 0 and ATTN_UNROLL_FAR % 2 == 0
    assert n_far % unroll_far == 0 and (n_steps - n_far) % ATTN_UNROLL == 0
    in_specs = [pl.BlockSpec(lam_rows.shape, lambda b, h, s: (0, 0)),
                pl.BlockSpec((2 * HEAD_DIM, seq), lambda b, h, s: (h, b)),
                pl.BlockSpec((seq, 2 * HEAD_DIM), lambda b, h, s: (b, h)),
                pl.BlockSpec((V_ROWS, seq), lambda b, h, s: (h, b)),
                pl.BlockSpec((None, t, 4 * t), lambda b, h, s: (h, 0, 0)),
                pl.BlockSpec(subln_g.shape, lambda b, h, s: (0, 0))]
    out_specs = pl.BlockSpec((seq, V_DIM), lambda b, h, s: (b, h))
    blocks = (3 * _nbytes((seq, 2 * HEAD_DIM), BF16) + _nbytes((V_ROWS, seq), BF16) + _nbytes((t, 4 * t), F32))
    scratch = [pltpu.VMEM((2, 2, t, t), F32),
               pltpu.VMEM((nq, 2, V_ROWS, t), F32),
               pltpu.VMEM((nq, 2, 1, t), F32)]
    temps = sum(_nbytes(s.shape, s.dtype) for s in scratch) + 8 * _nbytes((t, t), F32)
    return pl.pallas_call(
        functools.partial(_attn_kernel, lam_init=lam_init, n_far=n_far, n_steps=n_steps, unroll_far=unroll_far,
                          chain=bool(sched[2, n_far] == _KIND_FAR)),
        grid_spec=pltpu.PrefetchScalarGridSpec(
            num_scalar_prefetch=1, grid=(batch, N_ATT_HEADS), in_specs=in_specs, out_specs=out_specs,
            scratch_shapes=scratch),
        out_shape=jax.ShapeDtypeStruct((n, ATT_WIDTH), BF16),
        compiler_params=pltpu.CompilerParams(dimension_semantics=("arbitrary",) * 2,
                                             vmem_limit_bytes=_vmem_limit(blocks, temps)),
        name="diff_attn",
    )(jnp.asarray(sched), lam_rows, qt, k, vt, bias, subln_g)


def _merge_kernel(x_ref, g1_ref, attn_ref, sgu_ref, wg_ref, wpa_ref, wps_ref, wo_ref, h_ref):
    tm, d = x_ref.shape
    rows = tm // MERGE_SPLIT
    parts = []
    for a in range(MERGE_SPLIT):
        sl = slice(a * rows, (a + 1) * rows)
        x = x_ref[sl, :]
        xn = (x * _rms_scale(x) * g1_ref[...]).astype(BF16)
        gate = jax.nn.sigmoid(jnp.dot(xn, wg_ref[...], preferred_element_type=F32))
        y_attn = jnp.dot(attn_ref[sl, :], wpa_ref[...], preferred_element_type=F32)
        y_sgu = jnp.dot(sgu_ref[sl, :], wps_ref[...], preferred_element_type=F32)
        parts.append((sl, x, (gate[:, :d] * y_attn + gate[:, d:] * y_sgu).astype(BF16)))
    for sl, x, merged in parts:
        h_ref[sl, :] = x + jnp.dot(merged, wo_ref[...], preferred_element_type=F32)


def _merge(x2, g1, attn, sgu, wg, wpa, wps, wo):
    n, d = x2.shape
    tm = TOKEN_TILE
    row = lambda w: pl.BlockSpec((tm, w), lambda i: (i, 0))
    blocks = 2 * _nbytes((tm, d), F32) + 2 * _nbytes((tm, ATT_WIDTH), BF16)
    resident = sum(_nbytes(a.shape, a.dtype) for a in (g1, wg, wpa, wps, wo))
    temps = 8 * _nbytes((tm, d), F32)
    return pl.pallas_call(
        _merge_kernel, grid=(n // tm,),
        in_specs=[row(d), _resident(g1), row(ATT_WIDTH), row(SGU_WIDTH),
                  _resident(wg), _resident(wpa), _resident(wps), _resident(wo)],
        out_specs=row(d), out_shape=jax.ShapeDtypeStruct((n, d), F32),
        compiler_params=pltpu.CompilerParams(dimension_semantics=("arbitrary",),
                                             vmem_limit_bytes=_vmem_limit(blocks, temps, resident)),
        name="merge",
    )(x2, g1, attn, sgu, wg, wpa, wps, wo)


def _ffn_kernel(h_ref, g2_ref, w1_ref, w2_ref, gf_ref, o_ref, *, final_norm):
    h = h_ref[...]
    hn = (h * _rms_scale(h) * g2_ref[...]).astype(BF16)
    n_chunks = w1_ref.shape[1] // FF_CHUNK
    up = lambda c: jnp.dot(hn, w1_ref[:, c * FF_CHUNK:(c + 1) * FF_CHUNK], preferred_element_type=F32)
    out = h
    z = up(0)
    for c in range(n_chunks):
        z_next = up(c + 1) if c + 1 < n_chunks else None
        ff = jnp.square(jnp.maximum(z, 0.0)).astype(BF16)
        out = out + jnp.dot(ff, w2_ref[c * FF_CHUNK:(c + 1) * FF_CHUNK, :], preferred_element_type=F32)
        z = z_next
    if final_norm:
        out = out * _rms_scale(out) * gf_ref[...]
    o_ref[...] = out


def _ffn(h, g2, w1, w2, gf, final_norm):
    n, d = h.shape
    tm = TOKEN_TILE
    row = pl.BlockSpec((tm, d), lambda i: (i, 0))
    blocks = 2 * _nbytes((tm, d), F32)
    resident = sum(_nbytes(a.shape, a.dtype) for a in (g2, w1, w2, gf))
    temps = 4 * _nbytes((tm, d), F32) + 3 * _nbytes((tm, FF_CHUNK), F32)
    return pl.pallas_call(
        functools.partial(_ffn_kernel, final_norm=final_norm), grid=(n // tm,),
        in_specs=[row, _resident(g2), _resident(w1), _resident(w2), _resident(gf)],
        out_specs=row, out_shape=jax.ShapeDtypeStruct((n, d), F32),
        compiler_params=pltpu.CompilerParams(dimension_semantics=("arbitrary",),
                                             vmem_limit_bytes=_vmem_limit(blocks, temps, resident)),
        name="ffn",
    )(h, g2, w1, w2, gf)


def _layer(h, layer_idx, batch, seq, norm1_g, w_in, lam_rows, subln_g, bias, sgu_norm_g, w_spatial,
           b_spatial, w_proj_attn, w_proj_sgu, w_out, norm2_g, w_ff1, w_ff2, normf_g, final_norm):
    lam_init = 0.8 - 0.6 * math.exp(-0.3 * layer_idx)
    o_q, o_k, o_v, o_uv = QK_WIDTH, 2 * QK_WIDTH, 2 * QK_WIDTH + ATT_WIDTH, 2 * QK_WIDTH + ATT_WIDTH + 2 * SGU_WIDTH
    g1 = norm1_g[None, :]
    wqt = w_in[:, :o_q].T.astype(BF16)
    wk = w_in[:, o_q:o_k].astype(BF16)
    wvt = w_in[:, o_k:o_v].T.astype(BF16)
    wuv = w_in[:, o_v:o_uv].astype(BF16)
    wg = w_in[:, o_uv:].astype(BF16)
    bs_tile = jnp.repeat(b_spatial.T, SGU_GROUP_DIM, axis=1)
    k, qt, vt, sgu = _in_proj(h, g1, wk, wqt, wvt, wuv, sgu_norm_g, w_spatial, bs_tile)
    attn = _diff_attn(qt, k, vt, bias, lam_rows, subln_g[None, :], batch, seq, lam_init)
    h = _merge(h, g1, attn, sgu, wg, w_proj_attn.astype(BF16), w_proj_sgu.astype(BF16), w_out.astype(BF16))
    return _ffn(h, norm2_g[None, :], w_ff1.astype(BF16), w_ff2.astype(BF16), normf_g[None, :], final_norm)


def kernel(x, norm1_g, w_in, lam_q1, lam_k1, lam_q2, lam_k2, subln_g, rel_bias, sgu_norm_g, w_spatial,
           b_spatial, w_proj_attn, w_proj_sgu, w_out, norm2_g, w_ff1, w_ff2, normf_g):
    batch, seq, d = x.shape
    depth = w_in.shape[0]
    bias = _bias_tiles(rel_bias, seq, ATTN_TILE)
    h = x.reshape(batch * seq, d)
    for l in range(depth):
        lam_rows = jnp.stack([lam_q1[l], lam_k1[l], lam_q2[l], lam_k2[l]])
        h = _layer(h, l, batch, seq, norm1_g[l], w_in[l], lam_rows, subln_g[l], bias, sgu_norm_g[l],
                   w_spatial[l], b_spatial[l], w_proj_attn[l], w_proj_sgu[l], w_out[l], norm2_g[l],
                   w_ff1[l], w_ff2[l], normf_g, l == depth - 1)
    return h.reshape(batch, seq, d)
```

```python
import functools
import math

import numpy as np

import jax
import jax.numpy as jnp
from jax import lax
from jax.experimental import pallas as pl
from jax.experimental.pallas import tpu as pltpu

F32 = jnp.float32
BF16 = jnp.bfloat16

CHUNK = 64
N_ATT_HEADS = 4
HEAD_DIM = 64
V_DIM = 2 * HEAD_DIM
QK_WIDTH = N_ATT_HEADS * 2 * HEAD_DIM
ATT_WIDTH = N_ATT_HEADS * V_DIM
N_SGU_GROUPS = 4
SGU_GROUP_DIM = 128
SGU_WIDTH = N_SGU_GROUPS * SGU_GROUP_DIM
SGU_CHUNK = 128
N_BUCKETS = 32
MAX_DISTANCE = 128
NORM_EPS = 1e-6
ATTN_SCALE = HEAD_DIM ** -0.5
LOG2_E = math.log2(math.e)
V_ROWS = V_DIM + 16

V7X_VMEM_BYTES = 64 * 1024 * 1024
V7X_LANES = 128

TOKEN_TILE = 1024
ATTN_TILE = 512
ATTN_UNROLL = 8
ATTN_UNROLL_FAR = 26
MERGE_SPLIT = 2
BIAS_SUB = 128
FF_CHUNK = 1024

_NT_DIMS = (((1,), (1,)), ((), ()))


def _vmem_limit(block_bytes, temp_bytes, resident_bytes=0):
    return int(min(2 * block_bytes + resident_bytes + temp_bytes, V7X_VMEM_BYTES))


def _resident(a):
    return pl.BlockSpec(a.shape, lambda i: (0,) * a.ndim, pipeline_mode=pl.Buffered(1))


def _nbytes(shape, dtype):
    return math.prod(shape) * jnp.dtype(dtype).itemsize


def _rms_scale(x):
    return lax.rsqrt(jnp.mean(x * x, axis=-1, keepdims=True) + NORM_EPS)


def _in_proj_kernel(x_ref, g1_ref, wk_ref, wqt_ref, wvt_ref, wuv_ref, sgn_ref, ws_ref, bs_ref,
                    k_ref, qt_ref, vt_ref, sgu_ref):
    x = x_ref[...]
    xn = (x * _rms_scale(x) * g1_ref[...]).astype(BF16)
    zuv = jnp.dot(xn, wuv_ref[...], preferred_element_type=F32)
    k_ref[...] = jnp.dot(xn, wk_ref[...], preferred_element_type=F32).astype(BF16)
    qt = lax.dot_general(wqt_ref[...], xn, _NT_DIMS, preferred_element_type=F32)
    qt_ref[...] = (qt * (ATTN_SCALE * LOG2_E)).astype(BF16)
    vt = lax.dot_general(wvt_ref[...], xn, _NT_DIMS, preferred_element_type=F32).astype(BF16)
    ones = jnp.ones((V_ROWS - V_DIM, x.shape[0]), BF16)
    for h in range(N_ATT_HEADS):
        vt_ref[h * V_ROWS:h * V_ROWS + V_DIM, :] = vt[h * V_DIM:(h + 1) * V_DIM]
        vt_ref[h * V_ROWS + V_DIM:(h + 1) * V_ROWS, :] = ones

    guv = 0.5 * zuv * (1.0 + lax.erf(zuv * (2.0 ** -0.5)))
    n_chunks = x.shape[0] // SGU_CHUNK
    t_idx = lax.broadcasted_iota(jnp.int32, (SGU_CHUNK, SGU_CHUNK), 0)
    s_idx = lax.broadcasted_iota(jnp.int32, (SGU_CHUNK, SGU_CHUNK), 1)
    for g in range(N_SGU_GROUPS):
        lo = g * SGU_GROUP_DIM
        u = guv[:, lo:lo + SGU_GROUP_DIM]
        v = guv[:, SGU_WIDTH + lo:SGU_WIDTH + lo + SGU_GROUP_DIM]
        vn = (v * _rms_scale(v) * sgn_ref[g:g + 1, :]).astype(BF16)
        w = jnp.where(s_idx <= t_idx, ws_ref[g], 0.0).astype(BF16)
        b = bs_ref[:, lo:lo + SGU_GROUP_DIM]
        chunks = jnp.concatenate([vn[c * SGU_CHUNK:(c + 1) * SGU_CHUNK, :] for c in range(n_chunks)], axis=1)
        mixed = jnp.dot(w, chunks, preferred_element_type=F32)
        for c in range(n_chunks):
            r0 = c * SGU_CHUNK
            gated = u[r0:r0 + SGU_CHUNK, :] * (mixed[:, c * SGU_GROUP_DIM:(c + 1) * SGU_GROUP_DIM] + b)
            sgu_ref[r0:r0 + SGU_CHUNK, lo:lo + SGU_GROUP_DIM] = gated.astype(BF16)


def _in_proj(x2, g1, wk, wqt, wvt, wuv, sgn, ws, bs_tile):
    n, d = x2.shape
    tm = TOKEN_TILE
    params = (g1, wk, wqt, wvt, wuv, sgn, ws, bs_tile)
    in_specs = [pl.BlockSpec((tm, d), lambda i: (i, 0))] + [_resident(a) for a in params]
    out_shape = (jax.ShapeDtypeStruct((n, QK_WIDTH), BF16),
                 jax.ShapeDtypeStruct((QK_WIDTH, n), BF16),
                 jax.ShapeDtypeStruct((N_ATT_HEADS * V_ROWS, n), BF16),
                 jax.ShapeDtypeStruct((n, SGU_WIDTH), BF16))
    out_specs = (pl.BlockSpec((tm, QK_WIDTH), lambda i: (i, 0)),
                 pl.BlockSpec((QK_WIDTH, tm), lambda i: (0, i)),
                 pl.BlockSpec((N_ATT_HEADS * V_ROWS, tm), lambda i: (0, i)),
                 pl.BlockSpec((tm, SGU_WIDTH), lambda i: (i, 0)))
    blocks = (_nbytes((tm, d), F32) + 3 * _nbytes((tm, QK_WIDTH), BF16) + _nbytes((N_ATT_HEADS * V_ROWS, tm), BF16))
    resident = sum(_nbytes(a.shape, a.dtype) for a in params)
    temps = _nbytes((tm, d), F32) + 6 * _nbytes((tm, 2 * SGU_WIDTH), F32)
    return pl.pallas_call(
        _in_proj_kernel, grid=(n // tm,), in_specs=in_specs, out_specs=out_specs, out_shape=out_shape,
        compiler_params=pltpu.CompilerParams(dimension_semantics=("arbitrary",),
                                             vmem_limit_bytes=_vmem_limit(blocks, temps, resident)),
        name="in_proj",
    )(x2, *params)


_KIND_DIAG, _KIND_PREV, _KIND_FAR = 0, 1, 2


def _attn_schedule(nq):
    far = [(i, j, _KIND_FAR) for i in range(nq) for j in range(i - 1)]
    assert far and len(far) % 2 == 1, "the schedule keeps exactly one far pair for the second phase"
    n_far = len(far) - 1
    unroll_far = next((u for u in (ATTN_UNROLL_FAR, ATTN_UNROLL, 4, 2) if n_far % u == 0), 2)
    near = [far[-1], (0, 0, _KIND_DIAG)]
    for i in range(1, nq):
        near += [(i, i - 1, _KIND_PREV), (i, i, _KIND_DIAG)]
    unroll_near = next(u for u in (ATTN_UNROLL, 4, 2) if len(near) % u == 0)
    seen, steps = set(), []
    for i, j, kind in far[:n_far] + near:
        steps.append((i, j, kind, int(i not in seen)))
        seen.add(i)
    return np.asarray(steps, np.int32).T, n_far, unroll_far, unroll_near


def _attn_kernel(sched_ref, lam_ref, qt_ref, k_ref, vt_ref, diag_ref, corner_ref, g_ref, o_ref,
                 s_ref, acc_ref, m_ref, *, lam_init, n_far, n_steps, unroll_far, unroll_near):
    t = ATTN_TILE
    hb = t // 2
    r = BIAS_SUB
    row_id = lax.broadcasted_iota(jnp.int32, (2 * HEAD_DIM, t), 0)
    acc_ref[...] = jnp.zeros_like(acc_ref)
    m_ref[...] = jnp.zeros_like(m_ref)
    lp = lam_ref[...]
    lam = (jnp.exp(jnp.sum(lp[0:1] * lp[1:2], axis=-1, keepdims=True))
           - jnp.exp(jnp.sum(lp[2:3] * lp[3:4], axis=-1, keepdims=True)) + lam_init)
    gain = g_ref[...] * (1.0 - lam_init)

    def masked_queries(n):
        i = sched_ref[0, n]
        qt = qt_ref[:, pl.ds(pl.multiple_of(i * t, t), t)]
        zero = jnp.zeros_like(qt)
        return jnp.where(row_id < HEAD_DIM, qt, zero), jnp.where(row_id >= HEAD_DIM, qt, zero)

    def scores(n, slot, with_corner):
        j, kind = sched_ref[1, n], sched_ref[2, n]
        kb = k_ref[pl.ds(pl.multiple_of(j * t, t), t), :]
        block_max = []
        for mp, qp in enumerate(masked_queries(n)):
            s = jnp.dot(kb, qp, preferred_element_type=F32)
            if with_corner:
                corner = s[t - r:, :r] + corner_ref[kind]
                s = jnp.concatenate([s[:t - r], jnp.concatenate([corner, s[t - r:, r:]], axis=1)], axis=0)
            s_ref[slot, mp] = s
            block_max.append(jnp.max(s, axis=0, keepdims=True))
        return tuple(block_max)

    def scores_diag(n, slot):
        i = sched_ref[0, n]
        kb = k_ref[pl.ds(pl.multiple_of(i * t, t), t), :]
        bias = diag_ref[...]
        block_max = []
        for mp, qp in enumerate(masked_queries(n)):
            top = jnp.dot(kb[:hb], qp, preferred_element_type=F32) + bias
            bottom = jnp.dot(kb[hb:], qp[:, hb:], preferred_element_type=F32) + bias[:, :hb]
            s_ref[slot, mp, :hb, :] = top
            s_ref[slot, mp, hb:, hb:] = bottom
            late = jnp.maximum(jnp.max(top[:, hb:], axis=0, keepdims=True), jnp.max(bottom, axis=0, keepdims=True))
            block_max.append(jnp.concatenate([jnp.max(top[:, :hb], axis=0, keepdims=True), late], axis=1))
        return tuple(block_max)

    def finish(n, slot, block_max, diagonal):
        i, j, first = sched_ref[0, n], sched_ref[1, n], sched_ref[3, n]
        vb = vt_ref[:, pl.ds(pl.multiple_of(j * t, t), t)]
        for mp in range(2):
            m_old = jnp.where(first == 1, -jnp.inf, m_ref[i, mp])
            m_new = jnp.maximum(m_old, block_max[mp])
            if diagonal:
                top = jnp.exp2(s_ref[slot, mp, :hb, :] - m_new).astype(BF16)
                bottom = jnp.exp2(s_ref[slot, mp, hb:, hb:] - m_new[:, hb:]).astype(BF16)
                early = jnp.dot(vb[:, :hb], top[:, :hb], preferred_element_type=F32)
                late = jnp.dot(vb, jnp.concatenate([top[:, hb:], bottom], axis=0), preferred_element_type=F32)
                pv = jnp.concatenate([early, late], axis=1)
            else:
                p = jnp.exp2(s_ref[slot, mp] - m_new).astype(BF16)
                pv = jnp.dot(vb, p, preferred_element_type=F32)
            acc_ref[i, mp] = jnp.exp2(m_old - m_new) * acc_ref[i, mp] + pv
            m_ref[i, mp] = m_new

    def far_body(it, block_max):
        for u in range(unroll_far):
            n = it * unroll_far + u
            next_max = scores(n + 1, (u + 1) % 2, False)
            finish(n, u % 2, block_max, False)
            block_max = next_max
        return block_max

    if n_far:
        block_max = lax.fori_loop(0, n_far // unroll_far, far_body, scores(0, 0, False))
    else:
        block_max = scores(0, 0, True)

    def near_body(it, block_max):
        for u in range(unroll_near):
            n = n_far + it * unroll_near + u
            if u % 2 == 0:
                next_max = scores_diag(n + 1, (u + 1) % 2)
                finish(n, u % 2, block_max, False)
            else:
                next_max = scores(jnp.minimum(n + 1, n_steps - 1), (u + 1) % 2, True)
                finish(n, u % 2, block_max, True)
            block_max = next_max
        return block_max

    lax.fori_loop(0, (n_steps - n_far) // unroll_near, near_body, block_max)

    for i in range(acc_ref.shape[0]):
        heads = [acc_ref[i, mp, :V_DIM, :] * (1.0 / acc_ref[i, mp, V_DIM:V_DIM + 1, :]) for mp in range(2)]
        o = heads[0] - lam * heads[1]
        o = o * lax.rsqrt(jnp.mean(o * o, axis=0, keepdims=True) + NORM_EPS)
        o_ref[i * t:(i + 1) * t, :] = (o.T * gain).astype(BF16)


def _t5_bucket(rel):
    half = N_BUCKETS // 2
    ret = (rel > 0).astype(jnp.int32) * half
    n = jnp.abs(rel)
    max_exact = half // 2
    nf = jnp.maximum(n, 1).astype(F32)
    large = max_exact + (jnp.log(nf / max_exact) / math.log(MAX_DISTANCE / max_exact)
                         * (half - max_exact)).astype(jnp.int32)
    large = jnp.minimum(large, half - 1)
    return ret + jnp.where(n < max_exact, n, large)


def _bias_tiles(rel_bias, seq, t):
    r = BIAS_SUB
    period = 3 * r
    table = rel_bias.astype(F32)
    dist = r - 1 - jnp.arange(period, dtype=jnp.int32)
    far = table[_t5_bucket(jnp.asarray(-(seq - 1), jnp.int32))]
    vec = (table[_t5_bucket(dist)] - far).T
    heads = vec.shape[0]
    band = jnp.tile(vec, (1, r))[:, :r * (period - 1)].reshape(heads, r, period - 1)[:, :, r - 1:3 * r - 1]
    a = jnp.arange(r, dtype=jnp.int32)[:, None]
    c = jnp.arange(r, dtype=jnp.int32)[None, :]
    on_diag = jnp.where(((a // CHUNK) <= (c // CHUNK))[None], band[:, :, :r] * LOG2_E, -jnp.inf)
    below = band[:, :, r:] * LOG2_E
    zero = jnp.zeros_like(below)
    hidden = jnp.full_like(below, -jnp.inf)

    def sub_tile(key_blk, query_blk):
        d = key_blk - query_blk
        return hidden if d > 0 else on_diag if d == 0 else below if d == -1 else zero

    diag = jnp.concatenate([jnp.concatenate([sub_tile(kb, qb) for qb in range(t // r)], axis=-1)
                            for kb in range(t // (2 * r))], axis=-2)
    corner = jnp.stack([zero, below, zero], axis=1)
    return diag, corner


def _diff_attn(qt, k, vt, bias, lam_rows, subln_g, batch, seq, lam_init):
    t = ATTN_TILE
    r = BIAS_SUB
    assert r >= MAX_DISTANCE and r % CHUNK == 0 and t % (2 * r) == 0 and seq % t == 0
    n = batch * seq
    nq = seq // t
    sched, n_far, unroll_far, unroll_near = _attn_schedule(nq)
    n_steps = sched.shape[1]
    assert unroll_far % 2 == 0 and unroll_near % 2 == 0
    diag, corner = bias
    in_specs = [pl.BlockSpec(lam_rows.shape, lambda b, h, s: (0, 0)),
                pl.BlockSpec((2 * HEAD_DIM, seq), lambda b, h, s: (h, b)),
                pl.BlockSpec((seq, 2 * HEAD_DIM), lambda b, h, s: (b, h)),
                pl.BlockSpec((V_ROWS, seq), lambda b, h, s: (h, b)),
                pl.BlockSpec((None, t // 2, t), lambda b, h, s: (h, 0, 0)),
                pl.BlockSpec((None, 3, r, r), lambda b, h, s: (h, 0, 0, 0)),
                pl.BlockSpec(subln_g.shape, lambda b, h, s: (0, 0))]
    out_specs = pl.BlockSpec((seq, V_DIM), lambda b, h, s: (b, h))
    blocks = (3 * _nbytes((seq, 2 * HEAD_DIM), BF16) + _nbytes((V_ROWS, seq), BF16)
              + _nbytes((t // 2, t), F32) + _nbytes((3, r, r), F32))
    scratch = [pltpu.VMEM((2, 2, t, t), F32),
               pltpu.VMEM((nq, 2, V_ROWS, t), F32),
               pltpu.VMEM((nq, 2, 1, t), F32)]
    temps = sum(_nbytes(s.shape, s.dtype) for s in scratch) + 8 * _nbytes((t, t), F32)
    return pl.pallas_call(
        functools.partial(_attn_kernel, lam_init=lam_init, n_far=n_far, n_steps=n_steps, unroll_far=unroll_far,
                          unroll_near=unroll_near),
        grid_spec=pltpu.PrefetchScalarGridSpec(
            num_scalar_prefetch=1, grid=(batch, N_ATT_HEADS), in_specs=in_specs, out_specs=out_specs,
            scratch_shapes=scratch),
        out_shape=jax.ShapeDtypeStruct((n, ATT_WIDTH), BF16),
        compiler_params=pltpu.CompilerParams(dimension_semantics=("arbitrary",) * 2,
                                             vmem_limit_bytes=_vmem_limit(blocks, temps)),
        name="diff_attn",
    )(jnp.asarray(sched), lam_rows, qt, k, vt, diag, corner, subln_g)


def _merge_kernel(x_ref, g1_ref, attn_ref, sgu_ref, wg_ref, wpa_ref, wps_ref, wo_ref, h_ref):
    tm, d = x_ref.shape
    rows = tm // MERGE_SPLIT
    parts = []
    for a in range(MERGE_SPLIT):
        sl = slice(a * rows, (a + 1) * rows)
        x = x_ref[sl, :]
        xn = (x * _rms_scale(x) * g1_ref[...]).astype(BF16)
        gate = jax.nn.sigmoid(jnp.dot(xn, wg_ref[...], preferred_element_type=F32))
        y_attn = jnp.dot(attn_ref[sl, :], wpa_ref[...], preferred_element_type=F32)
        y_sgu = jnp.dot(sgu_ref[sl, :], wps_ref[...], preferred_element_type=F32)
        parts.append((sl, x, (gate[:, :d] * y_attn + gate[:, d:] * y_sgu).astype(BF16)))
    for sl, x, merged in parts:
        h_ref[sl, :] = x + jnp.dot(merged, wo_ref[...], preferred_element_type=F32)


def _merge(x2, g1, attn, sgu, wg, wpa, wps, wo):
    n, d = x2.shape
    tm = TOKEN_TILE
    row = lambda w: pl.BlockSpec((tm, w), lambda i: (i, 0))
    blocks = 2 * _nbytes((tm, d), F32) + 2 * _nbytes((tm, ATT_WIDTH), BF16)
    resident = sum(_nbytes(a.shape, a.dtype) for a in (g1, wg, wpa, wps, wo))
    temps = 8 * _nbytes((tm, d), F32)
    return pl.pallas_call(
        _merge_kernel, grid=(n // tm,),
        in_specs=[row(d), _resident(g1), row(ATT_WIDTH), row(SGU_WIDTH),
                  _resident(wg), _resident(wpa), _resident(wps), _resident(wo)],
        out_specs=row(d), out_shape=jax.ShapeDtypeStruct((n, d), F32),
        compiler_params=pltpu.CompilerParams(dimension_semantics=("arbitrary",),
                                             vmem_limit_bytes=_vmem_limit(blocks, temps, resident)),
        name="merge",
    )(x2, g1, attn, sgu, wg, wpa, wps, wo)


def _ffn_kernel(h_ref, g2_ref, w1_ref, w2_ref, gf_ref, o_ref, *, final_norm):
    h = h_ref[...]
    hn = (h * _rms_scale(h) * g2_ref[...]).astype(BF16)
    n_chunks = w1_ref.shape[1] // FF_CHUNK
    up = lambda c: jnp.dot(hn, w1_ref[:, c * FF_CHUNK:(c + 1) * FF_CHUNK], preferred_element_type=F32)
    out = h
    z = up(0)
    for c in range(n_chunks):
        z_next = up(c + 1) if c + 1 < n_chunks else None
        ff = jnp.square(jnp.maximum(z, 0.0)).astype(BF16)
        out = out + jnp.dot(ff, w2_ref[c * FF_CHUNK:(c + 1) * FF_CHUNK, :], preferred_element_type=F32)
        z = z_next
    if final_norm:
        out = out * _rms_scale(out) * gf_ref[...]
    o_ref[...] = out


def _ffn(h, g2, w1, w2, gf, final_norm):
    n, d = h.shape
    tm = TOKEN_TILE
    row = pl.BlockSpec((tm, d), lambda i: (i, 0))
    blocks = 2 * _nbytes((tm, d), F32)
    resident = sum(_nbytes(a.shape, a.dtype) for a in (g2, w1, w2, gf))
    temps = 4 * _nbytes((tm, d), F32) + 3 * _nbytes((tm, FF_CHUNK), F32)
    return pl.pallas_call(
        functools.partial(_ffn_kernel, final_norm=final_norm), grid=(n // tm,),
        in_specs=[row, _resident(g2), _resident(w1), _resident(w2), _resident(gf)],
        out_specs=row, out_shape=jax.ShapeDtypeStruct((n, d), F32),
        compiler_params=pltpu.CompilerParams(dimension_semantics=("arbitrary",),
                                             vmem_limit_bytes=_vmem_limit(blocks, temps, resident)),
        name="ffn",
    )(h, g2, w1, w2, gf)


def _layer(h, layer_idx, batch, seq, norm1_g, w_in, lam_rows, subln_g, bias, sgu_norm_g, w_spatial,
           b_spatial, w_proj_attn, w_proj_sgu, w_out, norm2_g, w_ff1, w_ff2, normf_g, final_norm):
    lam_init = 0.8 - 0.6 * math.exp(-0.3 * layer_idx)
    o_q, o_k, o_v, o_uv = QK_WIDTH, 2 * QK_WIDTH, 2 * QK_WIDTH + ATT_WIDTH, 2 * QK_WIDTH + ATT_WIDTH + 2 * SGU_WIDTH
    g1 = norm1_g[None, :]
    wqt = w_in[:, :o_q].T.astype(BF16)
    wk = w_in[:, o_q:o_k].astype(BF16)
    wvt = w_in[:, o_k:o_v].T.astype(BF16)
    wuv = w_in[:, o_v:o_uv].astype(BF16)
    wg = w_in[:, o_uv:].astype(BF16)
    bs_tile = jnp.repeat(b_spatial.T, SGU_GROUP_DIM, axis=1)
    k, qt, vt, sgu = _in_proj(h, g1, wk, wqt, wvt, wuv, sgu_norm_g, w_spatial, bs_tile)
    attn = _diff_attn(qt, k, vt, bias, lam_rows, subln_g[None, :], batch, seq, lam_init)
    h = _merge(h, g1, attn, sgu, wg, w_proj_attn.astype(BF16), w_proj_sgu.astype(BF16), w_out.astype(BF16))
    return _ffn(h, norm2_g[None, :], w_ff1.astype(BF16), w_ff2.astype(BF16), normf_g[None, :], final_norm)


def kernel(x, norm1_g, w_in, lam_q1, lam_k1, lam_q2, lam_k2, subln_g, rel_bias, sgu_norm_g, w_spatial,
           b_spatial, w_proj_attn, w_proj_sgu, w_out, norm2_g, w_ff1, w_ff2, normf_g):
    batch, seq, d = x.shape
    depth = w_in.shape[0]
    bias = _bias_tiles(rel_bias, seq, ATTN_TILE)
    h = x.reshape(batch * seq, d)
    for l in range(depth):
        lam_rows = jnp.stack([lam_q1[l], lam_k1[l], lam_q2[l], lam_k2[l]])
        h = _layer(h, l, batch, seq, norm1_g[l], w_in[l], lam_rows, subln_g[l], bias, sgu_norm_g[l],
                   w_spatial[l], b_spatial[l], w_proj_attn[l], w_proj_sgu[l], w_out[l], norm2_g[l],
                   w_ff1[l], w_ff2[l], normf_g, l == depth - 1)
    return h.reshape(batch, seq, d)
```

```python
import functools
import math

import numpy as np

import jax
import jax.numpy as jnp
from jax import lax
from jax.experimental import pallas as pl
from jax.experimental.pallas import tpu as pltpu

F32 = jnp.float32
BF16 = jnp.bfloat16

CHUNK = 64
N_ATT_HEADS = 4
HEAD_DIM = 64
V_DIM = 2 * HEAD_DIM
QK_WIDTH = N_ATT_HEADS * 2 * HEAD_DIM
ATT_WIDTH = N_ATT_HEADS * V_DIM
N_SGU_GROUPS = 4
SGU_GROUP_DIM = 128
SGU_WIDTH = N_SGU_GROUPS * SGU_GROUP_DIM
SGU_CHUNK = 128
N_BUCKETS = 32
MAX_DISTANCE = 128
NORM_EPS = 1e-6
ATTN_SCALE = HEAD_DIM ** -0.5
LOG2_E = math.log2(math.e)
V_ROWS = V_DIM + 16

V7X_VMEM_BYTES = 64 * 1024 * 1024

TOKEN_TILE = 1024
ATTN_TILE = 512
ATTN_UNROLL = 8
ATTN_UNROLL_FAR = 26
MERGE_SPLIT = 2
BIAS_SUB = 128
FF_ROWS = 512

_NT_DIMS = (((1,), (1,)), ((), ()))


def _vmem_limit(block_bytes, temp_bytes, resident_bytes=0):
    return int(min(2 * block_bytes + resident_bytes + temp_bytes, V7X_VMEM_BYTES))


def _resident(a):
    return pl.BlockSpec(a.shape, lambda i: (0,) * a.ndim, pipeline_mode=pl.Buffered(1))


def _nbytes(shape, dtype):
    return math.prod(shape) * jnp.dtype(dtype).itemsize


def _rms_scale(x):
    return lax.rsqrt(jnp.mean(x * x, axis=-1, keepdims=True) + NORM_EPS)


def _in_proj_kernel(x_ref, g1_ref, wk_ref, wqt_ref, wvt_ref, wuv_ref, sgn_ref, ws_ref, bs_ref,
                    k_ref, qt_ref, vt_ref, sgu_ref):
    x = x_ref[...]
    xn = (x * _rms_scale(x) * g1_ref[...]).astype(BF16)
    zuv = jnp.dot(xn, wuv_ref[...], preferred_element_type=F32)
    k_ref[...] = jnp.dot(xn, wk_ref[...], preferred_element_type=F32).astype(BF16)
    qt = lax.dot_general(wqt_ref[...], xn, _NT_DIMS, preferred_element_type=F32)
    qt_ref[...] = (qt * (ATTN_SCALE * LOG2_E)).astype(BF16)
    vt = lax.dot_general(wvt_ref[...], xn, _NT_DIMS, preferred_element_type=F32).astype(BF16)
    ones = jnp.ones((V_ROWS - V_DIM, x.shape[0]), BF16)
    for h in range(N_ATT_HEADS):
        vt_ref[h * V_ROWS:h * V_ROWS + V_DIM, :] = vt[h * V_DIM:(h + 1) * V_DIM]
        vt_ref[h * V_ROWS + V_DIM:(h + 1) * V_ROWS, :] = ones

    guv = 0.5 * zuv * (1.0 + lax.erf(zuv * (2.0 ** -0.5)))
    n_chunks = x.shape[0] // SGU_CHUNK
    t_idx = lax.broadcasted_iota(jnp.int32, (SGU_CHUNK, SGU_CHUNK), 0)
    s_idx = lax.broadcasted_iota(jnp.int32, (SGU_CHUNK, SGU_CHUNK), 1)
    for g in range(N_SGU_GROUPS):
        lo = g * SGU_GROUP_DIM
        u = guv[:, lo:lo + SGU_GROUP_DIM]
        v = guv[:, SGU_WIDTH + lo:SGU_WIDTH + lo + SGU_GROUP_DIM]
        vn = (v * _rms_scale(v) * sgn_ref[g:g + 1, :]).astype(BF16)
        w = jnp.where(s_idx <= t_idx, ws_ref[g], 0.0).astype(BF16)
        b = bs_ref[:, lo:lo + SGU_GROUP_DIM]
        chunks = jnp.concatenate([vn[c * SGU_CHUNK:(c + 1) * SGU_CHUNK, :] for c in range(n_chunks)], axis=1)
        mixed = jnp.dot(w, chunks, preferred_element_type=F32)
        for c in range(n_chunks):
            r0 = c * SGU_CHUNK
            gated = u[r0:r0 + SGU_CHUNK, :] * (mixed[:, c * SGU_GROUP_DIM:(c + 1) * SGU_GROUP_DIM] + b)
            sgu_ref[r0:r0 + SGU_CHUNK, lo:lo + SGU_GROUP_DIM] = gated.astype(BF16)


def _in_proj(x2, g1, wk, wqt, wvt, wuv, sgn, ws, bs_tile):
    n, d = x2.shape
    tm = TOKEN_TILE
    params = (g1, wk, wqt, wvt, wuv, sgn, ws, bs_tile)
    in_specs = [pl.BlockSpec((tm, d), lambda i: (i, 0))] + [_resident(a) for a in params]
    out_shape = (jax.ShapeDtypeStruct((n, QK_WIDTH), BF16),
                 jax.ShapeDtypeStruct((QK_WIDTH, n), BF16),
                 jax.ShapeDtypeStruct((N_ATT_HEADS * V_ROWS, n), BF16),
                 jax.ShapeDtypeStruct((n, SGU_WIDTH), BF16))
    out_specs = (pl.BlockSpec((tm, QK_WIDTH), lambda i: (i, 0)),
                 pl.BlockSpec((QK_WIDTH, tm), lambda i: (0, i)),
                 pl.BlockSpec((N_ATT_HEADS * V_ROWS, tm), lambda i: (0, i)),
                 pl.BlockSpec((tm, SGU_WIDTH), lambda i: (i, 0)))
    blocks = (_nbytes((tm, d), F32) + 3 * _nbytes((tm, QK_WIDTH), BF16) + _nbytes((N_ATT_HEADS * V_ROWS, tm), BF16))
    resident = sum(_nbytes(a.shape, a.dtype) for a in params)
    temps = _nbytes((tm, d), F32) + 6 * _nbytes((tm, 2 * SGU_WIDTH), F32)
    return pl.pallas_call(
        _in_proj_kernel, grid=(n // tm,), in_specs=in_specs, out_specs=out_specs, out_shape=out_shape,
        compiler_params=pltpu.CompilerParams(dimension_semantics=("arbitrary",),
                                             vmem_limit_bytes=_vmem_limit(blocks, temps, resident)),
        name="in_proj",
    )(x2, *params)


_KIND_DIAG, _KIND_PREV, _KIND_FAR = 0, 1, 2


def _attn_schedule(nq):
    far = [(i, j, _KIND_FAR) for i in range(nq) for j in range(i - 1)]
    assert far and len(far) % 2 == 1, "the schedule keeps exactly one far pair for the second phase"
    n_far = len(far) - 1
    unroll_far = next((u for u in (ATTN_UNROLL_FAR, ATTN_UNROLL, 4, 2) if n_far % u == 0), 2)
    near = [far[-1], (0, 0, _KIND_DIAG)]
    for i in range(1, nq):
        near += [(i, i - 1, _KIND_PREV), (i, i, _KIND_DIAG)]
    unroll_near = next(u for u in (ATTN_UNROLL, 4, 2) if len(near) % u == 0)
    seen, steps = set(), []
    for i, j, kind in far[:n_far] + near:
        steps.append((i, j, kind, int(i not in seen)))
        seen.add(i)
    return np.asarray(steps, np.int32).T, n_far, unroll_far, unroll_near


def _attn_kernel(sched_ref, lam_ref, qt_ref, k_ref, vt_ref, diag_ref, corner_ref, g_ref, o_ref,
                 s_ref, acc_ref, m_ref, *, lam_init, n_far, n_steps, unroll_far, unroll_near):
    t = ATTN_TILE
    hb = t // 2
    r = BIAS_SUB
    row_id = lax.broadcasted_iota(jnp.int32, (2 * HEAD_DIM, t), 0)
    acc_ref[...] = jnp.zeros_like(acc_ref)
    m_ref[...] = jnp.zeros_like(m_ref)
    lp = lam_ref[...]
    lam = (jnp.exp(jnp.sum(lp[0:1] * lp[1:2], axis=-1, keepdims=True))
           - jnp.exp(jnp.sum(lp[2:3] * lp[3:4], axis=-1, keepdims=True)) + lam_init)
    gain = g_ref[...] * (1.0 - lam_init)

    def masked_queries(n):
        i = sched_ref[0, n]
        qt = qt_ref[:, pl.ds(pl.multiple_of(i * t, t), t)]
        zero = jnp.zeros_like(qt)
        return jnp.where(row_id < HEAD_DIM, qt, zero), jnp.where(row_id >= HEAD_DIM, qt, zero)

    def scores(n, slot, with_corner):
        j, kind = sched_ref[1, n], sched_ref[2, n]
        kb = k_ref[pl.ds(pl.multiple_of(j * t, t), t), :]
        block_max = []
        for mp, qp in enumerate(masked_queries(n)):
            s = jnp.dot(kb, qp, preferred_element_type=F32)
            if with_corner:
                corner = s[t - r:, :r] + corner_ref[kind]
                s = jnp.concatenate([s[:t - r], jnp.concatenate([corner, s[t - r:, r:]], axis=1)], axis=0)
            s_ref[slot, mp] = s
            block_max.append(jnp.max(s, axis=0, keepdims=True))
        return tuple(block_max)

    def scores_diag(n, slot):
        i = sched_ref[0, n]
        kb = k_ref[pl.ds(pl.multiple_of(i * t, t), t), :]
        bias = diag_ref[...]
        block_max = []
        for mp, qp in enumerate(masked_queries(n)):
            top = jnp.dot(kb[:hb], qp, preferred_element_type=F32) + bias
            bottom = jnp.dot(kb[hb:], qp[:, hb:], preferred_element_type=F32) + bias[:, :hb]
            s_ref[slot, mp, :hb, :] = top
            s_ref[slot, mp, hb:, hb:] = bottom
            late = jnp.maximum(jnp.max(top[:, hb:], axis=0, keepdims=True), jnp.max(bottom, axis=0, keepdims=True))
            block_max.append(jnp.concatenate([jnp.max(top[:, :hb], axis=0, keepdims=True), late], axis=1))
        return tuple(block_max)

    def finish(n, slot, block_max, diagonal):
        i, j, first = sched_ref[0, n], sched_ref[1, n], sched_ref[3, n]
        vb = vt_ref[:, pl.ds(pl.multiple_of(j * t, t), t)]
        for mp in range(2):
            m_old = jnp.where(first == 1, -jnp.inf, m_ref[i, mp])
            m_new = jnp.maximum(m_old, block_max[mp])
            if diagonal:
                top = jnp.exp2(s_ref[slot, mp, :hb, :] - m_new).astype(BF16)
                bottom = jnp.exp2(s_ref[slot, mp, hb:, hb:] - m_new[:, hb:]).astype(BF16)
                early = jnp.dot(vb[:, :hb], top[:, :hb], preferred_element_type=F32)
                late = jnp.dot(vb, jnp.concatenate([top[:, hb:], bottom], axis=0), preferred_element_type=F32)
                pv = jnp.concatenate([early, late], axis=1)
            else:
                p = jnp.exp2(s_ref[slot, mp] - m_new).astype(BF16)
                pv = jnp.dot(vb, p, preferred_element_type=F32)
            acc_ref[i, mp] = jnp.exp2(m_old - m_new) * acc_ref[i, mp] + pv
            m_ref[i, mp] = m_new

    def far_body(it, block_max):
        for u in range(unroll_far):
            n = it * unroll_far + u
            next_max = scores(n + 1, (u + 1) % 2, False)
            finish(n, u % 2, block_max, False)
            block_max = next_max
        return block_max

    if n_far:
        block_max = lax.fori_loop(0, n_far // unroll_far, far_body, scores(0, 0, False))
    else:
        block_max = scores(0, 0, True)

    def near_body(it, block_max):
        for u in range(unroll_near):
            n = n_far + it * unroll_near + u
            if u % 2 == 0:
                next_max = scores_diag(n + 1, (u + 1) % 2)
                finish(n, u % 2, block_max, False)
            else:
                next_max = scores(jnp.minimum(n + 1, n_steps - 1), (u + 1) % 2, True)
                finish(n, u % 2, block_max, True)
            block_max = next_max
        return block_max

    lax.fori_loop(0, (n_steps - n_far) // unroll_near, near_body, block_max)

    for i in range(acc_ref.shape[0]):
        heads = [acc_ref[i, mp, :V_DIM, :] * (1.0 / acc_ref[i, mp, V_DIM:V_DIM + 1, :]) for mp in range(2)]
        o = heads[0] - lam * heads[1]
        o = o * lax.rsqrt(jnp.mean(o * o, axis=0, keepdims=True) + NORM_EPS)
        o_ref[i * t:(i + 1) * t, :] = (o.T * gain).astype(BF16)


def _t5_bucket(rel):
    half = N_BUCKETS // 2
    ret = (rel > 0).astype(jnp.int32) * half
    n = jnp.abs(rel)
    max_exact = half // 2
    nf = jnp.maximum(n, 1).astype(F32)
    large = max_exact + (jnp.log(nf / max_exact) / math.log(MAX_DISTANCE / max_exact)
                         * (half - max_exact)).astype(jnp.int32)
    large = jnp.minimum(large, half - 1)
    return ret + jnp.where(n < max_exact, n, large)


def _bias_tiles(rel_bias, seq, t):
    r = BIAS_SUB
    period = 3 * r
    table = rel_bias.astype(F32)
    dist = r - 1 - jnp.arange(period, dtype=jnp.int32)
    far = table[_t5_bucket(jnp.asarray(-(seq - 1), jnp.int32))]
    vec = (table[_t5_bucket(dist)] - far).T
    heads = vec.shape[0]
    band = jnp.tile(vec, (1, r))[:, :r * (period - 1)].reshape(heads, r, period - 1)[:, :, r - 1:3 * r - 1]
    a = jnp.arange(r, dtype=jnp.int32)[:, None]
    c = jnp.arange(r, dtype=jnp.int32)[None, :]
    on_diag = jnp.where(((a // CHUNK) <= (c // CHUNK))[None], band[:, :, :r] * LOG2_E, -jnp.inf)
    below = band[:, :, r:] * LOG2_E
    zero = jnp.zeros_like(below)
    hidden = jnp.full_like(below, -jnp.inf)

    def sub_tile(key_blk, query_blk):
        d = key_blk - query_blk
        return hidden if d > 0 else on_diag if d == 0 else below if d == -1 else zero

    diag = jnp.concatenate([jnp.concatenate([sub_tile(kb, qb) for qb in range(t // r)], axis=-1)
                            for kb in range(t // (2 * r))], axis=-2)
    corner = jnp.stack([zero, below, zero], axis=1)
    return diag, corner


def _diff_attn(qt, k, vt, bias, lam_rows, subln_g, batch, seq, lam_init):
    t = ATTN_TILE
    r = BIAS_SUB
    assert r >= MAX_DISTANCE and r % CHUNK == 0 and t % (2 * r) == 0 and seq % t == 0
    n = batch * seq
    nq = seq // t
    sched, n_far, unroll_far, unroll_near = _attn_schedule(nq)
    n_steps = sched.shape[1]
    assert unroll_far % 2 == 0 and unroll_near % 2 == 0
    diag, corner = bias
    in_specs = [pl.BlockSpec(lam_rows.shape, lambda b, h, s: (0, 0)),
                pl.BlockSpec((2 * HEAD_DIM, seq), lambda b, h, s: (h, b)),
                pl.BlockSpec((seq, 2 * HEAD_DIM), lambda b, h, s: (b, h)),
                pl.BlockSpec((V_ROWS, seq), lambda b, h, s: (h, b)),
                pl.BlockSpec((None, t // 2, t), lambda b, h, s: (h, 0, 0)),
                pl.BlockSpec((None, 3, r, r), lambda b, h, s: (h, 0, 0, 0)),
                pl.BlockSpec(subln_g.shape, lambda b, h, s: (0, 0))]
    out_specs = pl.BlockSpec((seq, V_DIM), lambda b, h, s: (b, h))
    blocks = (3 * _nbytes((seq, 2 * HEAD_DIM), BF16) + _nbytes((V_ROWS, seq), BF16)
              + _nbytes((t // 2, t), F32) + _nbytes((3, r, r), F32))
    scratch = [pltpu.VMEM((2, 2, t, t), F32),
               pltpu.VMEM((nq, 2, V_ROWS, t), F32),
               pltpu.VMEM((nq, 2, 1, t), F32)]
    temps = sum(_nbytes(s.shape, s.dtype) for s in scratch) + 8 * _nbytes((t, t), F32)
    return pl.pallas_call(
        functools.partial(_attn_kernel, lam_init=lam_init, n_far=n_far, n_steps=n_steps, unroll_far=unroll_far,
                          unroll_near=unroll_near),
        grid_spec=pltpu.PrefetchScalarGridSpec(
            num_scalar_prefetch=1, grid=(batch, N_ATT_HEADS), in_specs=in_specs, out_specs=out_specs,
            scratch_shapes=scratch),
        out_shape=jax.ShapeDtypeStruct((n, ATT_WIDTH), BF16),
        compiler_params=pltpu.CompilerParams(dimension_semantics=("arbitrary",) * 2,
                                             vmem_limit_bytes=_vmem_limit(blocks, temps)),
        name="diff_attn",
    )(jnp.asarray(sched), lam_rows, qt, k, vt, diag, corner, subln_g)


def _merge_kernel(x_ref, g1_ref, attn_ref, sgu_ref, wg_ref, wpa_ref, wps_ref, wo_ref, h_ref):
    tm, d = x_ref.shape
    rows = tm // MERGE_SPLIT
    parts = []
    for a in range(MERGE_SPLIT):
        sl = slice(a * rows, (a + 1) * rows)
        x = x_ref[sl, :]
        xn = (x * _rms_scale(x) * g1_ref[...]).astype(BF16)
        gate = jax.nn.sigmoid(jnp.dot(xn, wg_ref[...], preferred_element_type=F32))
        y_attn = jnp.dot(attn_ref[sl, :], wpa_ref[...], preferred_element_type=F32)
        y_sgu = jnp.dot(sgu_ref[sl, :], wps_ref[...], preferred_element_type=F32)
        parts.append((sl, x, (gate[:, :d] * y_attn + gate[:, d:] * y_sgu).astype(BF16)))
    for sl, x, merged in parts:
        h_ref[sl, :] = x + jnp.dot(merged, wo_ref[...], preferred_element_type=F32)


def _merge(x2, g1, attn, sgu, wg, wpa, wps, wo):
    n, d = x2.shape
    tm = TOKEN_TILE
    row = lambda w: pl.BlockSpec((tm, w), lambda i: (i, 0))
    blocks = 2 * _nbytes((tm, d), F32) + 2 * _nbytes((tm, ATT_WIDTH), BF16)
    resident = sum(_nbytes(a.shape, a.dtype) for a in (g1, wg, wpa, wps, wo))
    temps = 8 * _nbytes((tm, d), F32)
    return pl.pallas_call(
        _merge_kernel, grid=(n // tm,),
        in_specs=[row(d), _resident(g1), row(ATT_WIDTH), row(SGU_WIDTH),
                  _resident(wg), _resident(wpa), _resident(wps), _resident(wo)],
        out_specs=row(d), out_shape=jax.ShapeDtypeStruct((n, d), F32),
        compiler_params=pltpu.CompilerParams(dimension_semantics=("arbitrary",),
                                             vmem_limit_bytes=_vmem_limit(blocks, temps, resident)),
        name="merge",
    )(x2, g1, attn, sgu, wg, wpa, wps, wo)


def _ffn_kernel(h_ref, g2_ref, w1_ref, w2_ref, gf_ref, o_ref, *, final_norm):
    n_parts = h_ref.shape[0] // FF_ROWS
    rows = lambda a: slice(a * FF_ROWS, (a + 1) * FF_ROWS)

    def up(a):
        h = h_ref[rows(a), :]
        hn = (h * _rms_scale(h) * g2_ref[...]).astype(BF16)
        return jnp.dot(hn, w1_ref[...], preferred_element_type=F32)

    z = up(0)
    for a in range(n_parts):
        z_next = up(a + 1) if a + 1 < n_parts else None
        ff = jnp.square(jnp.maximum(z, 0.0)).astype(BF16)
        out = h_ref[rows(a), :] + jnp.dot(ff, w2_ref[...], preferred_element_type=F32)
        if final_norm:
            out = out * _rms_scale(out) * gf_ref[...]
        o_ref[rows(a), :] = out
        z = z_next


def _ffn(h, g2, w1, w2, gf, final_norm):
    n, d = h.shape
    tm = TOKEN_TILE
    row = pl.BlockSpec((tm, d), lambda i: (i, 0))
    blocks = 2 * _nbytes((tm, d), F32)
    resident = sum(_nbytes(a.shape, a.dtype) for a in (g2, w1, w2, gf))
    temps = 4 * _nbytes((tm, d), F32) + 3 * _nbytes((FF_ROWS, w1.shape[1]), F32)
    return pl.pallas_call(
        functools.partial(_ffn_kernel, final_norm=final_norm), grid=(n // tm,),
        in_specs=[row, _resident(g2), _resident(w1), _resident(w2), _resident(gf)],
        out_specs=row, out_shape=jax.ShapeDtypeStruct((n, d), F32),
        compiler_params=pltpu.CompilerParams(dimension_semantics=("arbitrary",),
                                             vmem_limit_bytes=_vmem_limit(blocks, temps, resident)),
        name="ffn",
    )(h, g2, w1, w2, gf)


def _layer(h, layer_idx, batch, seq, norm1_g, w_in, lam_rows, subln_g, bias, sgu_norm_g, w_spatial,
           b_spatial, w_proj_attn, w_proj_sgu, w_out, norm2_g, w_ff1, w_ff2, normf_g, final_norm):
    lam_init = 0.8 - 0.6 * math.exp(-0.3 * layer_idx)
    o_q, o_k, o_v, o_uv = QK_WIDTH, 2 * QK_WIDTH, 2 * QK_WIDTH + ATT_WIDTH, 2 * QK_WIDTH + ATT_WIDTH + 2 * SGU_WIDTH
    g1 = norm1_g[None, :]
    wqt = w_in[:, :o_q].T.astype(BF16)
    wk = w_in[:, o_q:o_k].astype(BF16)
    wvt = w_in[:, o_k:o_v].T.astype(BF16)
    wuv = w_in[:, o_v:o_uv].astype(BF16)
    wg = w_in[:, o_uv:].astype(BF16)
    bs_tile = jnp.repeat(b_spatial.T, SGU_GROUP_DIM, axis=1)
    k, qt, vt, sgu = _in_proj(h, g1, wk, wqt, wvt, wuv, sgu_norm_g, w_spatial, bs_tile)
    attn = _diff_attn(qt, k, vt, bias, lam_rows, subln_g[None, :], batch, seq, lam_init)
    h = _merge(h, g1, attn, sgu, wg, w_proj_attn.astype(BF16), w_proj_sgu.astype(BF16), w_out.astype(BF16))
    return _ffn(h, norm2_g[None, :], w_ff1.astype(BF16), w_ff2.astype(BF16), normf_g[None, :], final_norm)


def kernel(x, norm1_g, w_in, lam_q1, lam_k1, lam_q2, lam_k2, subln_g, rel_bias, sgu_norm_g, w_spatial,
           b_spatial, w_proj_attn, w_proj_sgu, w_out, norm2_g, w_ff1, w_ff2, normf_g):
    batch, seq, d = x.shape
    depth = w_in.shape[0]
    bias = _bias_tiles(rel_bias, seq, ATTN_TILE)
    h = x.reshape(batch * seq, d)
    for l in range(depth):
        lam_rows = jnp.stack([lam_q1[l], lam_k1[l], lam_q2[l], lam_k2[l]])
        h = _layer(h, l, batch, seq, norm1_g[l], w_in[l], lam_rows, subln_g[l], bias, sgu_norm_g[l],
                   w_spatial[l], b_spatial[l], w_proj_attn[l], w_proj_sgu[l], w_out[l], norm2_g[l],
                   w_ff1[l], w_ff2[l], normf_g, l == depth - 1)
    return h.reshape(batch, seq, d)
```

```python
import functools
import math

import numpy as np

import jax
import jax.numpy as jnp
from jax import lax
from jax.experimental import pallas as pl
from jax.experimental.pallas import tpu as pltpu

F32 = jnp.float32
BF16 = jnp.bfloat16

CHUNK = 64
N_ATT_HEADS = 4
HEAD_DIM = 64
V_DIM = 2 * HEAD_DIM
QK_WIDTH = N_ATT_HEADS * 2 * HEAD_DIM
ATT_WIDTH = N_ATT_HEADS * V_DIM
N_SGU_GROUPS = 4
SGU_GROUP_DIM = 128
SGU_WIDTH = N_SGU_GROUPS * SGU_GROUP_DIM
SGU_CHUNK = 128
N_BUCKETS = 32
MAX_DISTANCE = 128
NORM_EPS = 1e-6
ATTN_SCALE = HEAD_DIM ** -0.5
LOG2_E = math.log2(math.e)
V_ROWS = V_DIM + 16

V7X_VMEM_BYTES = 64 * 1024 * 1024

TOKEN_TILE = 1024
ATTN_TILE = 512
ATTN_UNROLL = 8
ATTN_UNROLL_FAR = 26
MERGE_SPLIT = 2
BIAS_SUB = 128
FF_ROWS = 512

_NT_DIMS = (((1,), (1,)), ((), ()))


def _vmem_limit(block_bytes, temp_bytes, resident_bytes=0):
    return int(min(2 * block_bytes + resident_bytes + temp_bytes, V7X_VMEM_BYTES))


def _resident(a):
    return pl.BlockSpec(a.shape, lambda i: (0,) * a.ndim, pipeline_mode=pl.Buffered(1))


def _nbytes(shape, dtype):
    return math.prod(shape) * jnp.dtype(dtype).itemsize


def _rms_scale(x):
    return lax.rsqrt(jnp.mean(x * x, axis=-1, keepdims=True) + NORM_EPS)


def _in_proj_kernel(x_ref, g1_ref, wk_ref, wqt_ref, wvt_ref, wuv_ref, sgn_ref, ws_ref, bs_ref,
                    k_ref, qt_ref, vt_ref, sgu_ref):
    x = x_ref[...]
    xn = (x * _rms_scale(x) * g1_ref[...]).astype(BF16)
    zuv = jnp.dot(xn, wuv_ref[...], preferred_element_type=F32)
    k_ref[...] = jnp.dot(xn, wk_ref[...], preferred_element_type=F32).astype(BF16)
    qt = lax.dot_general(wqt_ref[...], xn, _NT_DIMS, preferred_element_type=F32)
    qt_ref[...] = (qt * (ATTN_SCALE * LOG2_E)).astype(BF16)
    vt = lax.dot_general(wvt_ref[...], xn, _NT_DIMS, preferred_element_type=F32).astype(BF16)
    ones = jnp.ones((V_ROWS - V_DIM, x.shape[0]), BF16)
    for h in range(N_ATT_HEADS):
        vt_ref[h * V_ROWS:h * V_ROWS + V_DIM, :] = vt[h * V_DIM:(h + 1) * V_DIM]
        vt_ref[h * V_ROWS + V_DIM:(h + 1) * V_ROWS, :] = ones

    guv = 0.5 * zuv * (1.0 + lax.erf(zuv * (2.0 ** -0.5)))
    n_chunks = x.shape[0] // SGU_CHUNK
    t_idx = lax.broadcasted_iota(jnp.int32, (SGU_CHUNK, SGU_CHUNK), 0)
    s_idx = lax.broadcasted_iota(jnp.int32, (SGU_CHUNK, SGU_CHUNK), 1)
    for g in range(N_SGU_GROUPS):
        lo = g * SGU_GROUP_DIM
        u = guv[:, lo:lo + SGU_GROUP_DIM]
        v = guv[:, SGU_WIDTH + lo:SGU_WIDTH + lo + SGU_GROUP_DIM]
        vn = (v * _rms_scale(v) * sgn_ref[g:g + 1, :]).astype(BF16)
        w = jnp.where(s_idx <= t_idx, ws_ref[g], 0.0).astype(BF16)
        b = bs_ref[:, lo:lo + SGU_GROUP_DIM]
        chunks = jnp.concatenate([vn[c * SGU_CHUNK:(c + 1) * SGU_CHUNK, :] for c in range(n_chunks)], axis=1)
        mixed = jnp.dot(w, chunks, preferred_element_type=F32)
        for c in range(n_chunks):
            r0 = c * SGU_CHUNK
            gated = u[r0:r0 + SGU_CHUNK, :] * (mixed[:, c * SGU_GROUP_DIM:(c + 1) * SGU_GROUP_DIM] + b)
            sgu_ref[r0:r0 + SGU_CHUNK, lo:lo + SGU_GROUP_DIM] = gated.astype(BF16)


def _in_proj(x2, g1, wk, wqt, wvt, wuv, sgn, ws, bs_tile):
    n, d = x2.shape
    tm = TOKEN_TILE
    params = (g1, wk, wqt, wvt, wuv, sgn, ws, bs_tile)
    in_specs = [pl.BlockSpec((tm, d), lambda i: (i, 0))] + [_resident(a) for a in params]
    out_shape = (jax.ShapeDtypeStruct((n, QK_WIDTH), BF16),
                 jax.ShapeDtypeStruct((QK_WIDTH, n), BF16),
                 jax.ShapeDtypeStruct((N_ATT_HEADS * V_ROWS, n), BF16),
                 jax.ShapeDtypeStruct((n, SGU_WIDTH), BF16))
    out_specs = (pl.BlockSpec((tm, QK_WIDTH), lambda i: (i, 0)),
                 pl.BlockSpec((QK_WIDTH, tm), lambda i: (0, i)),
                 pl.BlockSpec((N_ATT_HEADS * V_ROWS, tm), lambda i: (0, i)),
                 pl.BlockSpec((tm, SGU_WIDTH), lambda i: (i, 0)))
    blocks = (_nbytes((tm, d), F32) + 3 * _nbytes((tm, QK_WIDTH), BF16) + _nbytes((N_ATT_HEADS * V_ROWS, tm), BF16))
    resident = sum(_nbytes(a.shape, a.dtype) for a in params)
    temps = _nbytes((tm, d), F32) + 6 * _nbytes((tm, 2 * SGU_WIDTH), F32)
    return pl.pallas_call(
        _in_proj_kernel, grid=(n // tm,), in_specs=in_specs, out_specs=out_specs, out_shape=out_shape,
        compiler_params=pltpu.CompilerParams(dimension_semantics=("arbitrary",),
                                             vmem_limit_bytes=_vmem_limit(blocks, temps, resident)),
        name="in_proj",
    )(x2, *params)


_KIND_DIAG, _KIND_PREV, _KIND_FAR = 0, 1, 2


def _attn_schedule(nq):
    far = [(i, j, _KIND_FAR) for i in range(nq) for j in range(i - 1)]
    assert far and len(far) % 2 == 1, "the schedule keeps exactly one far pair for the second phase"
    n_far = len(far) - 1
    unroll_far = next((u for u in (ATTN_UNROLL_FAR, ATTN_UNROLL, 4, 2) if n_far % u == 0), 2)
    near = [far[-1], (0, 0, _KIND_DIAG)]
    for i in range(1, nq):
        near += [(i, i - 1, _KIND_PREV), (i, i, _KIND_DIAG)]
    unroll_near = next(u for u in (ATTN_UNROLL, 4, 2) if len(near) % u == 0)
    seen, steps = set(), []
    for i, j, kind in far[:n_far] + near:
        steps.append((i, j, kind, int(i not in seen)))
        seen.add(i)
    return np.asarray(steps, np.int32).T, n_far, unroll_far, unroll_near


def _attn_kernel(sched_ref, lam_ref, qt_ref, k_ref, vt_ref, diag_ref, corner_ref, g_ref, o_ref,
                 s_ref, acc_ref, m_ref, *, lam_init, n_far, n_steps, unroll_far, unroll_near):
    t = ATTN_TILE
    hb = t // 2
    r = BIAS_SUB
    row_id = lax.broadcasted_iota(jnp.int32, (2 * HEAD_DIM, t), 0)
    acc_ref[...] = jnp.zeros_like(acc_ref)
    m_ref[...] = jnp.zeros_like(m_ref)
    lp = lam_ref[...]
    lam = (jnp.exp(jnp.sum(lp[0:1] * lp[1:2], axis=-1, keepdims=True))
           - jnp.exp(jnp.sum(lp[2:3] * lp[3:4], axis=-1, keepdims=True)) + lam_init)
    gain = g_ref[...] * (1.0 - lam_init)

    def masked_queries(n):
        i = sched_ref[0, n]
        qt = qt_ref[:, pl.ds(pl.multiple_of(i * t, t), t)]
        zero = jnp.zeros_like(qt)
        return jnp.where(row_id < HEAD_DIM, qt, zero), jnp.where(row_id >= HEAD_DIM, qt, zero)

    def scores(n, slot, with_corner):
        j, kind = sched_ref[1, n], sched_ref[2, n]
        kb = k_ref[pl.ds(pl.multiple_of(j * t, t), t), :]
        block_max = []
        for mp, qp in enumerate(masked_queries(n)):
            s = jnp.dot(kb, qp, preferred_element_type=F32)
            if with_corner:
                corner = s[t - r:, :r] + corner_ref[kind]
                s = jnp.concatenate([s[:t - r], jnp.concatenate([corner, s[t - r:, r:]], axis=1)], axis=0)
            s_ref[slot, mp, :, :t] = s
            block_max.append(jnp.max(s, axis=0, keepdims=True))
        return tuple(block_max)

    def scores_diag(n, slot):
        i = sched_ref[0, n]
        kb = k_ref[pl.ds(pl.multiple_of(i * t, t), t), :]
        bias = diag_ref[...]
        block_max = []
        for mp, qp in enumerate(masked_queries(n)):
            top = jnp.dot(kb[:hb], qp, preferred_element_type=F32) + bias
            bottom = jnp.dot(kb[hb:], qp[:, hb:], preferred_element_type=F32) + bias[:, :hb]
            s_ref[slot, mp, :hb, :t] = top
            s_ref[slot, mp, hb:, hb:t] = bottom
            late = jnp.maximum(jnp.max(top[:, hb:], axis=0, keepdims=True), jnp.max(bottom, axis=0, keepdims=True))
            block_max.append(jnp.concatenate([jnp.max(top[:, :hb], axis=0, keepdims=True), late], axis=1))
        return tuple(block_max)

    def finish(n, slot, block_max, diagonal):
        i, j, first = sched_ref[0, n], sched_ref[1, n], sched_ref[3, n]
        vb = vt_ref[:, pl.ds(pl.multiple_of(j * t, t), t)]
        for mp in range(2):
            m_old = jnp.where(first == 1, -jnp.inf, m_ref[i, mp])
            m_new = jnp.maximum(m_old, block_max[mp])
            if diagonal:
                top = jnp.exp2(s_ref[slot, mp, :hb, :t] - m_new).astype(BF16)
                bottom = jnp.exp2(s_ref[slot, mp, hb:, hb:t] - m_new[:, hb:]).astype(BF16)
                early = jnp.dot(vb[:, :hb], top[:, :hb], preferred_element_type=F32)
                late = jnp.dot(vb, jnp.concatenate([top[:, hb:], bottom], axis=0), preferred_element_type=F32)
                pv = jnp.concatenate([early, late], axis=1)
            else:
                p = jnp.exp2(s_ref[slot, mp, :, :t] - m_new).astype(BF16)
                pv = jnp.dot(vb, p, preferred_element_type=F32)
            acc_ref[i, mp] = jnp.exp2(m_old - m_new) * acc_ref[i, mp] + pv
            m_ref[i, mp] = m_new

    def far_body(it, block_max):
        for u in range(unroll_far):
            n = it * unroll_far + u
            next_max = scores(n + 1, (u + 1) % 2, False)
            finish(n, u % 2, block_max, False)
            block_max = next_max
        return block_max

    if n_far:
        block_max = lax.fori_loop(0, n_far // unroll_far, far_body, scores(0, 0, False))
    else:
        block_max = scores(0, 0, True)

    def near_body(it, block_max):
        for u in range(unroll_near):
            n = n_far + it * unroll_near + u
            if u % 2 == 0:
                next_max = scores_diag(n + 1, (u + 1) % 2)
                finish(n, u % 2, block_max, False)
            else:
                next_max = scores(jnp.minimum(n + 1, n_steps - 1), (u + 1) % 2, True)
                finish(n, u % 2, block_max, True)
            block_max = next_max
        return block_max

    lax.fori_loop(0, (n_steps - n_far) // unroll_near, near_body, block_max)

    for i in range(acc_ref.shape[0]):
        heads = [acc_ref[i, mp, :V_DIM, :] * (1.0 / acc_ref[i, mp, V_DIM:V_DIM + 1, :]) for mp in range(2)]
        o = heads[0] - lam * heads[1]
        o = o * lax.rsqrt(jnp.mean(o * o, axis=0, keepdims=True) + NORM_EPS)
        o_ref[i * t:(i + 1) * t, :] = (o.T * gain).astype(BF16)


def _t5_bucket(rel):
    half = N_BUCKETS // 2
    ret = (rel > 0).astype(jnp.int32) * half
    n = jnp.abs(rel)
    max_exact = half // 2
    nf = jnp.maximum(n, 1).astype(F32)
    large = max_exact + (jnp.log(nf / max_exact) / math.log(MAX_DISTANCE / max_exact)
                         * (half - max_exact)).astype(jnp.int32)
    large = jnp.minimum(large, half - 1)
    return ret + jnp.where(n < max_exact, n, large)


def _bias_tiles(rel_bias, seq, t):
    r = BIAS_SUB
    period = 3 * r
    table = rel_bias.astype(F32)
    dist = r - 1 - jnp.arange(period, dtype=jnp.int32)
    far = table[_t5_bucket(jnp.asarray(-(seq - 1), jnp.int32))]
    vec = (table[_t5_bucket(dist)] - far).T
    heads = vec.shape[0]
    band = jnp.tile(vec, (1, r))[:, :r * (period - 1)].reshape(heads, r, period - 1)[:, :, r - 1:3 * r - 1]
    a = jnp.arange(r, dtype=jnp.int32)[:, None]
    c = jnp.arange(r, dtype=jnp.int32)[None, :]
    on_diag = jnp.where(((a // CHUNK) <= (c // CHUNK))[None], band[:, :, :r] * LOG2_E, -jnp.inf)
    below = band[:, :, r:] * LOG2_E
    zero = jnp.zeros_like(below)
    hidden = jnp.full_like(below, -jnp.inf)

    def sub_tile(key_blk, query_blk):
        d = key_blk - query_blk
        return hidden if d > 0 else on_diag if d == 0 else below if d == -1 else zero

    diag = jnp.concatenate([jnp.concatenate([sub_tile(kb, qb) for qb in range(t // r)], axis=-1)
                            for kb in range(t // (2 * r))], axis=-2)
    corner = jnp.stack([zero, below, zero], axis=1)
    return diag, corner


def _diff_attn(qt, k, vt, bias, lam_rows, subln_g, batch, seq, lam_init):
    t = ATTN_TILE
    r = BIAS_SUB
    assert r >= MAX_DISTANCE and r % CHUNK == 0 and t % (2 * r) == 0 and seq % t == 0
    n = batch * seq
    nq = seq // t
    sched, n_far, unroll_far, unroll_near = _attn_schedule(nq)
    n_steps = sched.shape[1]
    assert unroll_far % 2 == 0 and unroll_near % 2 == 0
    diag, corner = bias
    in_specs = [pl.BlockSpec(lam_rows.shape, lambda b, h, s: (0, 0)),
                pl.BlockSpec((2 * HEAD_DIM, seq), lambda b, h, s: (h, b)),
                pl.BlockSpec((seq, 2 * HEAD_DIM), lambda b, h, s: (b, h)),
                pl.BlockSpec((V_ROWS, seq), lambda b, h, s: (h, b)),
                pl.BlockSpec((None, t // 2, t), lambda b, h, s: (h, 0, 0)),
                pl.BlockSpec((None, 3, r, r), lambda b, h, s: (h, 0, 0, 0)),
                pl.BlockSpec(subln_g.shape, lambda b, h, s: (0, 0))]
    out_specs = pl.BlockSpec((seq, V_DIM), lambda b, h, s: (b, h))
    blocks = (3 * _nbytes((seq, 2 * HEAD_DIM), BF16) + _nbytes((V_ROWS, seq), BF16)
              + _nbytes((t // 2, t), F32) + _nbytes((3, r, r), F32))
    scratch = [pltpu.VMEM((2, 2, t, t + 128), F32),
               pltpu.VMEM((nq, 2, V_ROWS, t), F32),
               pltpu.VMEM((nq, 2, 1, t), F32)]
    temps = sum(_nbytes(s.shape, s.dtype) for s in scratch) + 8 * _nbytes((t, t), F32)
    return pl.pallas_call(
        functools.partial(_attn_kernel, lam_init=lam_init, n_far=n_far, n_steps=n_steps, unroll_far=unroll_far,
                          unroll_near=unroll_near),
        grid_spec=pltpu.PrefetchScalarGridSpec(
            num_scalar_prefetch=1, grid=(batch, N_ATT_HEADS), in_specs=in_specs, out_specs=out_specs,
            scratch_shapes=scratch),
        out_shape=jax.ShapeDtypeStruct((n, ATT_WIDTH), BF16),
        compiler_params=pltpu.CompilerParams(dimension_semantics=("arbitrary",) * 2,
                                             vmem_limit_bytes=_vmem_limit(blocks, temps)),
        name="diff_attn",
    )(jnp.asarray(sched), lam_rows, qt, k, vt, diag, corner, subln_g)


def _merge_kernel(x_ref, g1_ref, attn_ref, sgu_ref, wg_ref, wpa_ref, wps_ref, wo_ref, h_ref):
    tm, d = x_ref.shape
    rows = tm // MERGE_SPLIT
    parts = []
    for a in range(MERGE_SPLIT):
        sl = slice(a * rows, (a + 1) * rows)
        x = x_ref[sl, :]
        xn = (x * _rms_scale(x) * g1_ref[...]).astype(BF16)
        gate = jax.nn.sigmoid(jnp.dot(xn, wg_ref[...], preferred_element_type=F32))
        y_attn = jnp.dot(attn_ref[sl, :], wpa_ref[...], preferred_element_type=F32)
        y_sgu = jnp.dot(sgu_ref[sl, :], wps_ref[...], preferred_element_type=F32)
        parts.append((sl, x, (gate[:, :d] * y_attn + gate[:, d:] * y_sgu).astype(BF16)))
    for sl, x, merged in parts:
        h_ref[sl, :] = x + jnp.dot(merged, wo_ref[...], preferred_element_type=F32)


def _merge(x2, g1, attn, sgu, wg, wpa, wps, wo):
    n, d = x2.shape
    tm = TOKEN_TILE
    row = lambda w: pl.BlockSpec((tm, w), lambda i: (i, 0))
    blocks = 2 * _nbytes((tm, d), F32) + 2 * _nbytes((tm, ATT_WIDTH), BF16)
    resident = sum(_nbytes(a.shape, a.dtype) for a in (g1, wg, wpa, wps, wo))
    temps = 8 * _nbytes((tm, d), F32)
    return pl.pallas_call(
        _merge_kernel, grid=(n // tm,),
        in_specs=[row(d), _resident(g1), row(ATT_WIDTH), row(SGU_WIDTH),
                  _resident(wg), _resident(wpa), _resident(wps), _resident(wo)],
        out_specs=row(d), out_shape=jax.ShapeDtypeStruct((n, d), F32),
        compiler_params=pltpu.CompilerParams(dimension_semantics=("arbitrary",),
                                             vmem_limit_bytes=_vmem_limit(blocks, temps, resident)),
        name="merge",
    )(x2, g1, attn, sgu, wg, wpa, wps, wo)


def _ffn_kernel(h_ref, g2_ref, w1_ref, w2_ref, gf_ref, o_ref, *, final_norm):
    n_parts = h_ref.shape[0] // FF_ROWS
    rows = lambda a: slice(a * FF_ROWS, (a + 1) * FF_ROWS)

    def up(a):
        h = h_ref[rows(a), :]
        hn = (h * _rms_scale(h) * g2_ref[...]).astype(BF16)
        return jnp.dot(hn, w1_ref[...], preferred_element_type=F32)

    z = up(0)
    for a in range(n_parts):
        z_next = up(a + 1) if a + 1 < n_parts else None
        ff = jnp.square(jnp.maximum(z, 0.0)).astype(BF16)
        out = h_ref[rows(a), :] + jnp.dot(ff, w2_ref[...], preferred_element_type=F32)
        if final_norm:
            out = out * _rms_scale(out) * gf_ref[...]
        o_ref[rows(a), :] = out
        z = z_next


def _ffn(h, g2, w1, w2, gf, final_norm):
    n, d = h.shape
    tm = TOKEN_TILE
    row = pl.BlockSpec((tm, d), lambda i: (i, 0))
    blocks = 2 * _nbytes((tm, d), F32)
    resident = sum(_nbytes(a.shape, a.dtype) for a in (g2, w1, w2, gf))
    temps = 4 * _nbytes((tm, d), F32) + 3 * _nbytes((FF_ROWS, w1.shape[1]), F32)
    return pl.pallas_call(
        functools.partial(_ffn_kernel, final_norm=final_norm), grid=(n // tm,),
        in_specs=[row, _resident(g2), _resident(w1), _resident(w2), _resident(gf)],
        out_specs=row, out_shape=jax.ShapeDtypeStruct((n, d), F32),
        compiler_params=pltpu.CompilerParams(dimension_semantics=("arbitrary",),
                                             vmem_limit_bytes=_vmem_limit(blocks, temps, resident)),
        name="ffn",
    )(h, g2, w1, w2, gf)


def _layer(h, layer_idx, batch, seq, norm1_g, w_in, lam_rows, subln_g, bias, sgu_norm_g, w_spatial,
           b_spatial, w_proj_attn, w_proj_sgu, w_out, norm2_g, w_ff1, w_ff2, normf_g, final_norm):
    lam_init = 0.8 - 0.6 * math.exp(-0.3 * layer_idx)
    o_q, o_k, o_v, o_uv = QK_WIDTH, 2 * QK_WIDTH, 2 * QK_WIDTH + ATT_WIDTH, 2 * QK_WIDTH + ATT_WIDTH + 2 * SGU_WIDTH
    g1 = norm1_g[None, :]
    wqt = w_in[:, :o_q].T.astype(BF16)
    wk = w_in[:, o_q:o_k].astype(BF16)
    wvt = w_in[:, o_k:o_v].T.astype(BF16)
    wuv = w_in[:, o_v:o_uv].astype(BF16)
    wg = w_in[:, o_uv:].astype(BF16)
    bs_tile = jnp.repeat(b_spatial.T, SGU_GROUP_DIM, axis=1)
    k, qt, vt, sgu = _in_proj(h, g1, wk, wqt, wvt, wuv, sgu_norm_g, w_spatial, bs_tile)
    attn = _diff_attn(qt, k, vt, bias, lam_rows, subln_g[None, :], batch, seq, lam_init)
    h = _merge(h, g1, attn, sgu, wg, w_proj_attn.astype(BF16), w_proj_sgu.astype(BF16), w_out.astype(BF16))
    return _ffn(h, norm2_g[None, :], w_ff1.astype(BF16), w_ff2.astype(BF16), normf_g[None, :], final_norm)


def kernel(x, norm1_g, w_in, lam_q1, lam_k1, lam_q2, lam_k2, subln_g, rel_bias, sgu_norm_g, w_spatial,
           b_spatial, w_proj_attn, w_proj_sgu, w_out, norm2_g, w_ff1, w_ff2, normf_g):
    batch, seq, d = x.shape
    depth = w_in.shape[0]
    bias = _bias_tiles(rel_bias, seq, ATTN_TILE)
    h = x.reshape(batch * seq, d)
    for l in range(depth):
        lam_rows = jnp.stack([lam_q1[l], lam_k1[l], lam_q2[l], lam_k2[l]])
        h = _layer(h, l, batch, seq, norm1_g[l], w_in[l], lam_rows, subln_g[l], bias, sgu_norm_g[l],
                   w_spatial[l], b_spatial[l], w_proj_attn[l], w_proj_sgu[l], w_out[l], norm2_g[l],
                   w_ff1[l], w_ff2[l], normf_g, l == depth - 1)
    return h.reshape(batch, seq, d)
```

```python
import functools
import math

import numpy as np

import jax
import jax.numpy as jnp
from jax import lax
from jax.experimental import pallas as pl
from jax.experimental.pallas import tpu as pltpu

F32 = jnp.float32
BF16 = jnp.bfloat16

CHUNK = 64
N_ATT_HEADS = 4
HEAD_DIM = 64
V_DIM = 2 * HEAD_DIM
QK_WIDTH = N_ATT_HEADS * 2 * HEAD_DIM
ATT_WIDTH = N_ATT_HEADS * V_DIM
N_SGU_GROUPS = 4
SGU_GROUP_DIM = 128
SGU_WIDTH = N_SGU_GROUPS * SGU_GROUP_DIM
SGU_CHUNK = 128
N_BUCKETS = 32
MAX_DISTANCE = 128
NORM_EPS = 1e-6
ATTN_SCALE = HEAD_DIM ** -0.5
LOG2_E = math.log2(math.e)
V_ROWS = V_DIM + 16

V7X_VMEM_BYTES = 64 * 1024 * 1024

TOKEN_TILE = 1024
ATTN_TILE = 512
ATTN_UNROLL = 8
ATTN_UNROLL_FAR = 26
MERGE_SPLIT = 2
BIAS_SUB = 128
FF_ROWS = 512

_NT_DIMS = (((1,), (1,)), ((), ()))


def _vmem_limit(block_bytes, temp_bytes, resident_bytes=0):
    return int(min(2 * block_bytes + resident_bytes + temp_bytes, V7X_VMEM_BYTES))


def _resident(a):
    return pl.BlockSpec(a.shape, lambda i: (0,) * a.ndim, pipeline_mode=pl.Buffered(1))


def _nbytes(shape, dtype):
    return math.prod(shape) * jnp.dtype(dtype).itemsize


def _rms_scale(x):
    return lax.rsqrt(jnp.mean(x * x, axis=-1, keepdims=True) + NORM_EPS)


def _in_proj_kernel(x_ref, g1_ref, wk_ref, wqt_ref, wvt_ref, wuv_ref, sgn_ref, ws_ref, bs_ref, *refs, cast_cols):
    n_cast = len(cast_cols)
    k_ref, qt_ref, vt_ref, sgu_ref = refs[n_cast:n_cast + 4]
    for src_ref, dst_ref, col in zip(refs[:n_cast], refs[n_cast + 4:], cast_cols):
        dst_ref[...] = src_ref[:, col:].astype(BF16)

    x = x_ref[...]
    xn = (x * _rms_scale(x) * g1_ref[...]).astype(BF16)
    zuv = jnp.dot(xn, wuv_ref[...], preferred_element_type=F32)
    k_ref[...] = jnp.dot(xn, wk_ref[...], preferred_element_type=F32).astype(BF16)
    qt = lax.dot_general(wqt_ref[...], xn, _NT_DIMS, preferred_element_type=F32)
    qt_ref[...] = (qt * (ATTN_SCALE * LOG2_E)).astype(BF16)
    vt = lax.dot_general(wvt_ref[...], xn, _NT_DIMS, preferred_element_type=F32).astype(BF16)
    ones = jnp.ones((V_ROWS - V_DIM, x.shape[0]), BF16)
    for h in range(N_ATT_HEADS):
        vt_ref[h * V_ROWS:h * V_ROWS + V_DIM, :] = vt[h * V_DIM:(h + 1) * V_DIM]
        vt_ref[h * V_ROWS + V_DIM:(h + 1) * V_ROWS, :] = ones

    guv = 0.5 * zuv * (1.0 + lax.erf(zuv * (2.0 ** -0.5)))
    n_chunks = x.shape[0] // SGU_CHUNK
    t_idx = lax.broadcasted_iota(jnp.int32, (SGU_CHUNK, SGU_CHUNK), 0)
    s_idx = lax.broadcasted_iota(jnp.int32, (SGU_CHUNK, SGU_CHUNK), 1)
    for g in range(N_SGU_GROUPS):
        lo = g * SGU_GROUP_DIM
        u = guv[:, lo:lo + SGU_GROUP_DIM]
        v = guv[:, SGU_WIDTH + lo:SGU_WIDTH + lo + SGU_GROUP_DIM]
        vn = (v * _rms_scale(v) * sgn_ref[g:g + 1, :]).astype(BF16)
        w = jnp.where(s_idx <= t_idx, ws_ref[g], 0.0).astype(BF16)
        b = bs_ref[:, lo:lo + SGU_GROUP_DIM]
        chunks = jnp.concatenate([vn[c * SGU_CHUNK:(c + 1) * SGU_CHUNK, :] for c in range(n_chunks)], axis=1)
        mixed = jnp.dot(w, chunks, preferred_element_type=F32)
        for c in range(n_chunks):
            r0 = c * SGU_CHUNK
            gated = u[r0:r0 + SGU_CHUNK, :] * (mixed[:, c * SGU_GROUP_DIM:(c + 1) * SGU_GROUP_DIM] + b)
            sgu_ref[r0:r0 + SGU_CHUNK, lo:lo + SGU_GROUP_DIM] = gated.astype(BF16)


def _in_proj(x2, g1, wk, wqt, wvt, wuv, sgn, ws, bs_tile, later):
    n, d = x2.shape
    tm = TOKEN_TILE
    steps = n // tm
    params = (g1, wk, wqt, wvt, wuv, sgn, ws, bs_tile)
    cast_src = [w for w, _ in later]
    cast_cols = tuple(col for _, col in later)
    cast_rows = [w.shape[0] // steps for w in cast_src]
    assert all(w.shape[0] % steps == 0 and rows % 16 == 0 for w, rows in zip(cast_src, cast_rows))
    in_specs = ([pl.BlockSpec((tm, d), lambda i: (i, 0))] + [_resident(a) for a in params]
                + [pl.BlockSpec((rows, w.shape[1]), lambda i: (i, 0)) for w, rows in zip(cast_src, cast_rows)])
    out_shape = ([jax.ShapeDtypeStruct((n, QK_WIDTH), BF16),
                  jax.ShapeDtypeStruct((QK_WIDTH, n), BF16),
                  jax.ShapeDtypeStruct((N_ATT_HEADS * V_ROWS, n), BF16),
                  jax.ShapeDtypeStruct((n, SGU_WIDTH), BF16)]
                 + [jax.ShapeDtypeStruct((w.shape[0], w.shape[1] - col), BF16) for w, col in later])
    out_specs = ([pl.BlockSpec((tm, QK_WIDTH), lambda i: (i, 0)),
                  pl.BlockSpec((QK_WIDTH, tm), lambda i: (0, i)),
                  pl.BlockSpec((N_ATT_HEADS * V_ROWS, tm), lambda i: (0, i)),
                  pl.BlockSpec((tm, SGU_WIDTH), lambda i: (i, 0))]
                 + [pl.BlockSpec((rows, w.shape[1] - col), lambda i: (i, 0))
                    for (w, col), rows in zip(later, cast_rows)])
    blocks = (_nbytes((tm, d), F32) + 3 * _nbytes((tm, QK_WIDTH), BF16) + _nbytes((N_ATT_HEADS * V_ROWS, tm), BF16)
              + sum(_nbytes((rows, 2 * w.shape[1]), F32) for w, rows in zip(cast_src, cast_rows)))
    resident = sum(_nbytes(a.shape, a.dtype) for a in params)
    temps = _nbytes((tm, d), F32) + 6 * _nbytes((tm, 2 * SGU_WIDTH), F32)
    return pl.pallas_call(
        functools.partial(_in_proj_kernel, cast_cols=cast_cols),
        grid=(steps,), in_specs=in_specs, out_specs=out_specs, out_shape=out_shape,
        compiler_params=pltpu.CompilerParams(dimension_semantics=("arbitrary",),
                                             vmem_limit_bytes=_vmem_limit(blocks, temps, resident)),
        name="in_proj",
    )(x2, *params, *cast_src)


_KIND_DIAG, _KIND_PREV, _KIND_FAR = 0, 1, 2


def _attn_schedule(nq):
    far = [(i, j, _KIND_FAR) for i in range(nq) for j in range(i - 1)]
    assert far and len(far) % 2 == 1, "the schedule keeps exactly one far pair for the second phase"
    n_far = len(far) - 1
    unroll_far = next((u for u in (ATTN_UNROLL_FAR, ATTN_UNROLL, 4, 2) if n_far % u == 0), 2)
    near = [far[-1], (0, 0, _KIND_DIAG)]
    for i in range(1, nq):
        near += [(i, i - 1, _KIND_PREV), (i, i, _KIND_DIAG)]
    unroll_near = next(u for u in (ATTN_UNROLL, 4, 2) if len(near) % u == 0)
    seen, steps = set(), []
    for i, j, kind in far[:n_far] + near:
        steps.append((i, j, kind, int(i not in seen)))
        seen.add(i)
    return np.asarray(steps, np.int32).T, n_far, unroll_far, unroll_near


def _attn_kernel(sched_ref, lam_ref, qt_ref, k_ref, vt_ref, diag_ref, corner_ref, g_ref, o_ref,
                 s_ref, acc_ref, m_ref, *, lam_init, n_far, n_steps, unroll_far, unroll_near):
    t = ATTN_TILE
    hb = t // 2
    r = BIAS_SUB
    row_id = lax.broadcasted_iota(jnp.int32, (2 * HEAD_DIM, t), 0)
    acc_ref[...] = jnp.zeros_like(acc_ref)
    m_ref[...] = jnp.zeros_like(m_ref)
    lp = lam_ref[...]
    lam = (jnp.exp(jnp.sum(lp[0:1] * lp[1:2], axis=-1, keepdims=True))
           - jnp.exp(jnp.sum(lp[2:3] * lp[3:4], axis=-1, keepdims=True)) + lam_init)
    gain = g_ref[...] * (1.0 - lam_init)

    def masked_queries(n):
        i = sched_ref[0, n]
        qt = qt_ref[:, pl.ds(pl.multiple_of(i * t, t), t)]
        zero = jnp.zeros_like(qt)
        return jnp.where(row_id < HEAD_DIM, qt, zero), jnp.where(row_id >= HEAD_DIM, qt, zero)

    def scores(n, slot, with_corner):
        j, kind = sched_ref[1, n], sched_ref[2, n]
        kb = k_ref[pl.ds(pl.multiple_of(j * t, t), t), :]
        block_max = []
        for mp, qp in enumerate(masked_queries(n)):
            s = jnp.dot(kb, qp, preferred_element_type=F32)
            if with_corner:
                corner = s[t - r:, :r] + corner_ref[kind]
                s = jnp.concatenate([s[:t - r], jnp.concatenate([corner, s[t - r:, r:]], axis=1)], axis=0)
            s_ref[slot, mp] = s
            block_max.append(jnp.max(s, axis=0, keepdims=True))
        return tuple(block_max)

    def scores_diag(n, slot):
        i = sched_ref[0, n]
        kb = k_ref[pl.ds(pl.multiple_of(i * t, t), t), :]
        bias = diag_ref[...]
        block_max = []
        for mp, qp in enumerate(masked_queries(n)):
            top = jnp.dot(kb[:hb], qp, preferred_element_type=F32) + bias
            bottom = jnp.dot(kb[hb:], qp[:, hb:], preferred_element_type=F32) + bias[:, :hb]
            s_ref[slot, mp, :hb, :] = top
            s_ref[slot, mp, hb:, hb:] = bottom
            late = jnp.maximum(jnp.max(top[:, hb:], axis=0, keepdims=True), jnp.max(bottom, axis=0, keepdims=True))
            block_max.append(jnp.concatenate([jnp.max(top[:, :hb], axis=0, keepdims=True), late], axis=1))
        return tuple(block_max)

    def finish(n, slot, block_max, diagonal):
        i, j, first = sched_ref[0, n], sched_ref[1, n], sched_ref[3, n]
        vb = vt_ref[:, pl.ds(pl.multiple_of(j * t, t), t)]
        for mp in range(2):
            m_old = jnp.where(first == 1, -jnp.inf, m_ref[i, mp])
            m_new = jnp.maximum(m_old, block_max[mp])
            if diagonal:
                top = jnp.exp2(s_ref[slot, mp, :hb, :] - m_new).astype(BF16)
                bottom = jnp.exp2(s_ref[slot, mp, hb:, hb:] - m_new[:, hb:]).astype(BF16)
                early = jnp.dot(vb[:, :hb], top[:, :hb], preferred_element_type=F32)
                late = jnp.dot(vb, jnp.concatenate([top[:, hb:], bottom], axis=0), preferred_element_type=F32)
                pv = jnp.concatenate([early, late], axis=1)
            else:
                p = jnp.exp2(s_ref[slot, mp] - m_new).astype(BF16)
                pv = jnp.dot(vb, p, preferred_element_type=F32)
            acc_ref[i, mp] = jnp.exp2(m_old - m_new) * acc_ref[i, mp] + pv
            m_ref[i, mp] = m_new

    def far_body(it, block_max):
        for u in range(unroll_far):
            n = it * unroll_far + u
            next_max = scores(n + 1, (u + 1) % 2, False)
            finish(n, u % 2, block_max, False)
            block_max = next_max
        return block_max

    if n_far:
        block_max = lax.fori_loop(0, n_far // unroll_far, far_body, scores(0, 0, False))
    else:
        block_max = scores(0, 0, True)

    def near_body(it, block_max):
        for u in range(unroll_near):
            n = n_far + it * unroll_near + u
            if u % 2 == 0:
                next_max = scores_diag(n + 1, (u + 1) % 2)
                finish(n, u % 2, block_max, False)
            else:
                next_max = scores(jnp.minimum(n + 1, n_steps - 1), (u + 1) % 2, True)
                finish(n, u % 2, block_max, True)
            block_max = next_max
        return block_max

    lax.fori_loop(0, (n_steps - n_far) // unroll_near, near_body, block_max)

    for i in range(acc_ref.shape[0]):
        heads = [acc_ref[i, mp, :V_DIM, :] * (1.0 / acc_ref[i, mp, V_DIM:V_DIM + 1, :]) for mp in range(2)]
        o = heads[0] - lam * heads[1]
        o = o * lax.rsqrt(jnp.mean(o * o, axis=0, keepdims=True) + NORM_EPS)
        o_ref[i * t:(i + 1) * t, :] = (o.T * gain).astype(BF16)


def _t5_bucket(rel):
    half = N_BUCKETS // 2
    ret = (rel > 0).astype(jnp.int32) * half
    n = jnp.abs(rel)
    max_exact = half // 2
    nf = jnp.maximum(n, 1).astype(F32)
    large = max_exact + (jnp.log(nf / max_exact) / math.log(MAX_DISTANCE / max_exact)
                         * (half - max_exact)).astype(jnp.int32)
    large = jnp.minimum(large, half - 1)
    return ret + jnp.where(n < max_exact, n, large)


def _bias_tiles(rel_bias, seq, t):
    r = BIAS_SUB
    period = 3 * r
    table = rel_bias.astype(F32)
    dist = r - 1 - jnp.arange(period, dtype=jnp.int32)
    far = table[_t5_bucket(jnp.asarray(-(seq - 1), jnp.int32))]
    vec = (table[_t5_bucket(dist)] - far).T
    heads = vec.shape[0]
    band = jnp.tile(vec, (1, r))[:, :r * (period - 1)].reshape(heads, r, period - 1)[:, :, r - 1:3 * r - 1]
    a = jnp.arange(r, dtype=jnp.int32)[:, None]
    c = jnp.arange(r, dtype=jnp.int32)[None, :]
    on_diag = jnp.where(((a // CHUNK) <= (c // CHUNK))[None], band[:, :, :r] * LOG2_E, -jnp.inf)
    below = band[:, :, r:] * LOG2_E
    zero = jnp.zeros_like(below)
    hidden = jnp.full_like(below, -jnp.inf)

    def sub_tile(key_blk, query_blk):
        d = key_blk - query_blk
        return hidden if d > 0 else on_diag if d == 0 else below if d == -1 else zero

    diag = jnp.concatenate([jnp.concatenate([sub_tile(kb, qb) for qb in range(t // r)], axis=-1)
                            for kb in range(t // (2 * r))], axis=-2)
    corner = jnp.stack([zero, below, zero], axis=1)
    return diag, corner


def _diff_attn(qt, k, vt, bias, lam_rows, subln_g, batch, seq, lam_init):
    t = ATTN_TILE
    r = BIAS_SUB
    assert r >= MAX_DISTANCE and r % CHUNK == 0 and t % (2 * r) == 0 and seq % t == 0
    n = batch * seq
    nq = seq // t
    sched, n_far, unroll_far, unroll_near = _attn_schedule(nq)
    n_steps = sched.shape[1]
    assert unroll_far % 2 == 0 and unroll_near % 2 == 0
    diag, corner = bias
    in_specs = [pl.BlockSpec(lam_rows.shape, lambda b, h, s: (0, 0)),
                pl.BlockSpec((2 * HEAD_DIM, seq), lambda b, h, s: (h, b)),
                pl.BlockSpec((seq, 2 * HEAD_DIM), lambda b, h, s: (b, h)),
                pl.BlockSpec((V_ROWS, seq), lambda b, h, s: (h, b)),
                pl.BlockSpec((None, t // 2, t), lambda b, h, s: (h, 0, 0)),
                pl.BlockSpec((None, 3, r, r), lambda b, h, s: (h, 0, 0, 0)),
                pl.BlockSpec(subln_g.shape, lambda b, h, s: (0, 0))]
    out_specs = pl.BlockSpec((seq, V_DIM), lambda b, h, s: (b, h))
    blocks = (3 * _nbytes((seq, 2 * HEAD_DIM), BF16) + _nbytes((V_ROWS, seq), BF16)
              + _nbytes((t // 2, t), F32) + _nbytes((3, r, r), F32))
    scratch = [pltpu.VMEM((2, 2, t, t), F32),
               pltpu.VMEM((nq, 2, V_ROWS, t), F32),
               pltpu.VMEM((nq, 2, 1, t), F32)]
    temps = sum(_nbytes(s.shape, s.dtype) for s in scratch) + 8 * _nbytes((t, t), F32)
    return pl.pallas_call(
        functools.partial(_attn_kernel, lam_init=lam_init, n_far=n_far, n_steps=n_steps, unroll_far=unroll_far,
                          unroll_near=unroll_near),
        grid_spec=pltpu.PrefetchScalarGridSpec(
            num_scalar_prefetch=1, grid=(batch, N_ATT_HEADS), in_specs=in_specs, out_specs=out_specs,
            scratch_shapes=scratch),
        out_shape=jax.ShapeDtypeStruct((n, ATT_WIDTH), BF16),
        compiler_params=pltpu.CompilerParams(dimension_semantics=("arbitrary",) * 2,
                                             vmem_limit_bytes=_vmem_limit(blocks, temps)),
        name="diff_attn",
    )(jnp.asarray(sched), lam_rows, qt, k, vt, diag, corner, subln_g)


def _merge_kernel(x_ref, g1_ref, attn_ref, sgu_ref, wg_ref, wpa_ref, wps_ref, wo_ref, h_ref):
    tm, d = x_ref.shape
    rows = tm // MERGE_SPLIT
    parts = []
    for a in range(MERGE_SPLIT):
        sl = slice(a * rows, (a + 1) * rows)
        x = x_ref[sl, :]
        xn = (x * _rms_scale(x) * g1_ref[...]).astype(BF16)
        gate = jax.nn.sigmoid(jnp.dot(xn, wg_ref[...], preferred_element_type=F32))
        y_attn = jnp.dot(attn_ref[sl, :], wpa_ref[...], preferred_element_type=F32)
        y_sgu = jnp.dot(sgu_ref[sl, :], wps_ref[...], preferred_element_type=F32)
        parts.append((sl, x, (gate[:, :d] * y_attn + gate[:, d:] * y_sgu).astype(BF16)))
    for sl, x, merged in parts:
        h_ref[sl, :] = x + jnp.dot(merged, wo_ref[...], preferred_element_type=F32)


def _merge(x2, g1, attn, sgu, wg, wpa, wps, wo):
    n, d = x2.shape
    tm = TOKEN_TILE
    row = lambda w: pl.BlockSpec((tm, w), lambda i: (i, 0))
    blocks = 2 * _nbytes((tm, d), F32) + 2 * _nbytes((tm, ATT_WIDTH), BF16)
    resident = sum(_nbytes(a.shape, a.dtype) for a in (g1, wg, wpa, wps, wo))
    temps = 8 * _nbytes((tm, d), F32)
    return pl.pallas_call(
        _merge_kernel, grid=(n // tm,),
        in_specs=[row(d), _resident(g1), row(ATT_WIDTH), row(SGU_WIDTH),
                  _resident(wg), _resident(wpa), _resident(wps), _resident(wo)],
        out_specs=row(d), out_shape=jax.ShapeDtypeStruct((n, d), F32),
        compiler_params=pltpu.CompilerParams(dimension_semantics=("arbitrary",),
                                             vmem_limit_bytes=_vmem_limit(blocks, temps, resident)),
        name="merge",
    )(x2, g1, attn, sgu, wg, wpa, wps, wo)


def _ffn_kernel(h_ref, g2_ref, w1_ref, w2_ref, gf_ref, o_ref, *, final_norm):
    n_parts = h_ref.shape[0] // FF_ROWS
    rows = lambda a: slice(a * FF_ROWS, (a + 1) * FF_ROWS)

    def up(a):
        h = h_ref[rows(a), :]
        hn = (h * _rms_scale(h) * g2_ref[...]).astype(BF16)
        return jnp.dot(hn, w1_ref[...], preferred_element_type=F32)

    z = up(0)
    for a in range(n_parts):
        z_next = up(a + 1) if a + 1 < n_parts else None
        ff = jnp.square(jnp.maximum(z, 0.0)).astype(BF16)
        out = h_ref[rows(a), :] + jnp.dot(ff, w2_ref[...], preferred_element_type=F32)
        if final_norm:
            out = out * _rms_scale(out) * gf_ref[...]
        o_ref[rows(a), :] = out
        z = z_next


def _ffn(h, g2, w1, w2, gf, final_norm):
    n, d = h.shape
    tm = TOKEN_TILE
    row = pl.BlockSpec((tm, d), lambda i: (i, 0))
    blocks = 2 * _nbytes((tm, d), F32)
    resident = sum(_nbytes(a.shape, a.dtype) for a in (g2, w1, w2, gf))
    temps = 4 * _nbytes((tm, d), F32) + 3 * _nbytes((FF_ROWS, w1.shape[1]), F32)
    return pl.pallas_call(
        functools.partial(_ffn_kernel, final_norm=final_norm), grid=(n // tm,),
        in_specs=[row, _resident(g2), _resident(w1), _resident(w2), _resident(gf)],
        out_specs=row, out_shape=jax.ShapeDtypeStruct((n, d), F32),
        compiler_params=pltpu.CompilerParams(dimension_semantics=("arbitrary",),
                                             vmem_limit_bytes=_vmem_limit(blocks, temps, resident)),
        name="ffn",
    )(h, g2, w1, w2, gf)


def _layer(h, layer_idx, batch, seq, norm1_g, w_in, lam_rows, subln_g, bias, sgu_norm_g, w_spatial,
           b_spatial, w_proj_attn, w_proj_sgu, w_out, norm2_g, w_ff1, w_ff2, normf_g, final_norm):
    lam_init = 0.8 - 0.6 * math.exp(-0.3 * layer_idx)
    o_q, o_k, o_v, o_uv = QK_WIDTH, 2 * QK_WIDTH, 2 * QK_WIDTH + ATT_WIDTH, 2 * QK_WIDTH + ATT_WIDTH + 2 * SGU_WIDTH
    g1 = norm1_g[None, :]
    wqt = w_in[:, :o_q].T.astype(BF16)
    wk = w_in[:, o_q:o_k].astype(BF16)
    wvt = w_in[:, o_k:o_v].T.astype(BF16)
    wuv = w_in[:, o_v:o_uv].astype(BF16)
    bs_tile = jnp.repeat(b_spatial.T, SGU_GROUP_DIM, axis=1)
    later = [(w_in, o_uv), (w_proj_attn, 0), (w_proj_sgu, 0), (w_out, 0), (w_ff1, 0), (w_ff2, 0)]
    k, qt, vt, sgu, wg, wpa, wps, wo, w1, w2 = _in_proj(h, g1, wk, wqt, wvt, wuv, sgu_norm_g, w_spatial,
                                                        bs_tile, later)
    attn = _diff_attn(qt, k, vt, bias, lam_rows, subln_g[None, :], batch, seq, lam_init)
    h = _merge(h, g1, attn, sgu, wg, wpa, wps, wo)
    return _ffn(h, norm2_g[None, :], w1, w2, normf_g[None, :], final_norm)


def kernel(x, norm1_g, w_in, lam_q1, lam_k1, lam_q2, lam_k2, subln_g, rel_bias, sgu_norm_g, w_spatial,
           b_spatial, w_proj_attn, w_proj_sgu, w_out, norm2_g, w_ff1, w_ff2, normf_g):
    batch, seq, d = x.shape
    depth = w_in.shape[0]
    bias = _bias_tiles(rel_bias, seq, ATTN_TILE)
    h = x.reshape(batch * seq, d)
    for l in range(depth):
        lam_rows = jnp.stack([lam_q1[l], lam_k1[l], lam_q2[l], lam_k2[l]])
        h = _layer(h, l, batch, seq, norm1_g[l], w_in[l], lam_rows, subln_g[l], bias, sgu_norm_g[l],
                   w_spatial[l], b_spatial[l], w_proj_attn[l], w_proj_sgu[l], w_out[l], norm2_g[l],
                   w_ff1[l], w_ff2[l], normf_g, l == depth - 1)
    return h.reshape(batch, seq, d)
```

```python
import functools
import math

import numpy as np

import jax
import jax.numpy as jnp
from jax import lax
from jax.experimental import pallas as pl
from jax.experimental.pallas import tpu as pltpu

F32 = jnp.float32
BF16 = jnp.bfloat16

CHUNK = 64
N_ATT_HEADS = 4
HEAD_DIM = 64
V_DIM = 2 * HEAD_DIM
QK_WIDTH = N_ATT_HEADS * 2 * HEAD_DIM
ATT_WIDTH = N_ATT_HEADS * V_DIM
N_SGU_GROUPS = 4
SGU_GROUP_DIM = 128
SGU_WIDTH = N_SGU_GROUPS * SGU_GROUP_DIM
SGU_CHUNK = 128
N_BUCKETS = 32
MAX_DISTANCE = 128
NORM_EPS = 1e-6
ATTN_SCALE = HEAD_DIM ** -0.5
LOG2_E = math.log2(math.e)
V_ROWS = V_DIM + 16

V7X_VMEM_BYTES = 64 * 1024 * 1024

TOKEN_TILE = 1024
ATTN_TILE = 512
ATTN_UNROLL = 8
ATTN_UNROLL_FAR = 26
MERGE_SPLIT = 2
BIAS_SUB = 128
FF_ROWS = 512

_NT_DIMS = (((1,), (1,)), ((), ()))


def _vmem_limit(block_bytes, temp_bytes, resident_bytes=0):
    return int(min(2 * block_bytes + resident_bytes + temp_bytes, V7X_VMEM_BYTES))


def _resident(a):
    return pl.BlockSpec(a.shape, lambda i: (0,) * a.ndim, pipeline_mode=pl.Buffered(1))


def _nbytes(shape, dtype):
    return math.prod(shape) * jnp.dtype(dtype).itemsize


def _rms_scale(x):
    return lax.rsqrt(jnp.mean(x * x, axis=-1, keepdims=True) + NORM_EPS)


def _in_proj_kernel(x_ref, g1_ref, wk_ref, wqt_ref, wvt_ref, wuv_ref, sgn_ref, ws_ref, bs_ref, *refs, cast_cols):
    n_cast = len(cast_cols)
    k_ref, qt_ref, vt_ref, sgu_ref = refs[n_cast:n_cast + 4]
    for src_ref, dst_ref, col in zip(refs[:n_cast], refs[n_cast + 4:], cast_cols):
        dst_ref[...] = src_ref[:, col:].astype(BF16)

    x = x_ref[...]
    xn = (x * _rms_scale(x) * g1_ref[...]).astype(BF16)
    zuv = jnp.dot(xn, wuv_ref[...], preferred_element_type=F32)
    k_ref[...] = jnp.dot(xn, wk_ref[...], preferred_element_type=F32).astype(BF16)
    qt = lax.dot_general(wqt_ref[...], xn, _NT_DIMS, preferred_element_type=F32)
    qt_ref[...] = (qt * (ATTN_SCALE * LOG2_E)).astype(BF16)
    vt = lax.dot_general(wvt_ref[...], xn, _NT_DIMS, preferred_element_type=F32).astype(BF16)
    ones = jnp.ones((V_ROWS - V_DIM, x.shape[0]), BF16)
    for h in range(N_ATT_HEADS):
        vt_ref[h * V_ROWS:h * V_ROWS + V_DIM, :] = vt[h * V_DIM:(h + 1) * V_DIM]
        vt_ref[h * V_ROWS + V_DIM:(h + 1) * V_ROWS, :] = ones

    guv = 0.5 * zuv * (1.0 + lax.erf(zuv * (2.0 ** -0.5)))
    n_chunks = x.shape[0] // SGU_CHUNK
    t_idx = lax.broadcasted_iota(jnp.int32, (SGU_CHUNK, SGU_CHUNK), 0)
    s_idx = lax.broadcasted_iota(jnp.int32, (SGU_CHUNK, SGU_CHUNK), 1)
    for g in range(N_SGU_GROUPS):
        lo = g * SGU_GROUP_DIM
        u = guv[:, lo:lo + SGU_GROUP_DIM]
        v = guv[:, SGU_WIDTH + lo:SGU_WIDTH + lo + SGU_GROUP_DIM]
        vn = (v * _rms_scale(v) * sgn_ref[g:g + 1, :]).astype(BF16)
        w = jnp.where(s_idx <= t_idx, ws_ref[g], 0.0).astype(BF16)
        b = bs_ref[:, lo:lo + SGU_GROUP_DIM]
        chunks = jnp.concatenate([vn[c * SGU_CHUNK:(c + 1) * SGU_CHUNK, :] for c in range(n_chunks)], axis=1)
        mixed = jnp.dot(w, chunks, preferred_element_type=F32)
        for c in range(n_chunks):
            r0 = c * SGU_CHUNK
            gated = u[r0:r0 + SGU_CHUNK, :] * (mixed[:, c * SGU_GROUP_DIM:(c + 1) * SGU_GROUP_DIM] + b)
            sgu_ref[r0:r0 + SGU_CHUNK, lo:lo + SGU_GROUP_DIM] = gated.astype(BF16)


def _in_proj(x2, g1, wk, wqt, wvt, wuv, sgn, ws, bs_tile, later):
    n, d = x2.shape
    tm = TOKEN_TILE
    steps = n // tm
    params = (g1, wk, wqt, wvt, wuv, sgn, ws, bs_tile)
    cast_src = [w for w, _ in later]
    cast_cols = tuple(col for _, col in later)
    cast_rows = [w.shape[0] // steps for w in cast_src]
    assert all(w.shape[0] % steps == 0 and rows % 16 == 0 for w, rows in zip(cast_src, cast_rows))
    in_specs = ([pl.BlockSpec((tm, d), lambda i: (i, 0))] + [_resident(a) for a in params]
                + [pl.BlockSpec((rows, w.shape[1]), lambda i: (i, 0)) for w, rows in zip(cast_src, cast_rows)])
    out_shape = ([jax.ShapeDtypeStruct((n, QK_WIDTH), BF16),
                  jax.ShapeDtypeStruct((QK_WIDTH, n), BF16),
                  jax.ShapeDtypeStruct((N_ATT_HEADS * V_ROWS, n), BF16),
                  jax.ShapeDtypeStruct((n, SGU_WIDTH), BF16)]
                 + [jax.ShapeDtypeStruct((w.shape[0], w.shape[1] - col), BF16) for w, col in later])
    out_specs = ([pl.BlockSpec((tm, QK_WIDTH), lambda i: (i, 0)),
                  pl.BlockSpec((QK_WIDTH, tm), lambda i: (0, i)),
                  pl.BlockSpec((N_ATT_HEADS * V_ROWS, tm), lambda i: (0, i)),
                  pl.BlockSpec((tm, SGU_WIDTH), lambda i: (i, 0))]
                 + [pl.BlockSpec((rows, w.shape[1] - col), lambda i: (i, 0))
                    for (w, col), rows in zip(later, cast_rows)])
    blocks = (_nbytes((tm, d), F32) + 3 * _nbytes((tm, QK_WIDTH), BF16) + _nbytes((N_ATT_HEADS * V_ROWS, tm), BF16)
              + sum(_nbytes((rows, 2 * w.shape[1]), F32) for w, rows in zip(cast_src, cast_rows)))
    resident = sum(_nbytes(a.shape, a.dtype) for a in params)
    temps = _nbytes((tm, d), F32) + 6 * _nbytes((tm, 2 * SGU_WIDTH), F32)
    return pl.pallas_call(
        functools.partial(_in_proj_kernel, cast_cols=cast_cols),
        grid=(steps,), in_specs=in_specs, out_specs=out_specs, out_shape=out_shape,
        compiler_params=pltpu.CompilerParams(dimension_semantics=("arbitrary",),
                                             vmem_limit_bytes=_vmem_limit(blocks, temps, resident)),
        name="in_proj",
    )(x2, *params, *cast_src)


_KIND_DIAG, _KIND_PREV, _KIND_FAR = 0, 1, 2


def _attn_schedule(nq):
    far = [(i, j, _KIND_FAR) for i in range(nq) for j in range(i - 1)]
    assert far and len(far) % 2 == 1, "the schedule keeps exactly one far pair for the second phase"
    n_far = len(far) - 1
    unroll_far = next((u for u in (ATTN_UNROLL_FAR, ATTN_UNROLL, 4, 2) if n_far % u == 0), 2)
    near = [far[-1], (0, 0, _KIND_DIAG)]
    for i in range(1, nq):
        near += [(i, i - 1, _KIND_PREV), (i, i, _KIND_DIAG)]
    unroll_near = next(u for u in (ATTN_UNROLL, 4, 2) if len(near) % u == 0)
    seen, steps = set(), []
    for i, j, kind in far[:n_far] + near:
        steps.append((i, j, kind, int(i not in seen)))
        seen.add(i)
    return np.asarray(steps, np.int32).T, n_far, unroll_far, unroll_near


def _attn_kernel(sched_ref, lam_ref, qt_ref, k_ref, vt_ref, diag_ref, corner_ref, g_ref, o_ref,
                 s_ref, acc_ref, m_ref, *, lam_init, n_far, n_steps, unroll_far, unroll_near):
    t = ATTN_TILE
    hb = t // 2
    r = BIAS_SUB
    row_id = lax.broadcasted_iota(jnp.int32, (2 * HEAD_DIM, t), 0)
    acc_ref[...] = jnp.zeros_like(acc_ref)
    m_ref[...] = jnp.zeros_like(m_ref)
    lp = lam_ref[...]
    lam = (jnp.exp(jnp.sum(lp[0:1] * lp[1:2], axis=-1, keepdims=True))
           - jnp.exp(jnp.sum(lp[2:3] * lp[3:4], axis=-1, keepdims=True)) + lam_init)
    gain = g_ref[...] * (1.0 - lam_init)

    def masked_queries(n):
        i = sched_ref[0, n]
        qt = qt_ref[:, pl.ds(pl.multiple_of(i * t, t), t)]
        zero = jnp.zeros_like(qt)
        return jnp.where(row_id < HEAD_DIM, qt, zero), jnp.where(row_id >= HEAD_DIM, qt, zero)

    def scores(n, slot, with_corner, maps=(0, 1)):
        j, kind = sched_ref[1, n], sched_ref[2, n]
        kb = k_ref[pl.ds(pl.multiple_of(j * t, t), t), :]
        block_max = []
        queries = masked_queries(n)
        for mp in maps:
            qp = queries[mp]
            s = jnp.dot(kb, qp, preferred_element_type=F32)
            if with_corner:
                corner = s[t - r:, :r] + corner_ref[kind]
                s = jnp.concatenate([s[:t - r], jnp.concatenate([corner, s[t - r:, r:]], axis=1)], axis=0)
            s_ref[slot, mp] = s
            block_max.append(jnp.max(s, axis=0, keepdims=True))
        return tuple(block_max)

    def scores_diag(n, slot, maps=(0, 1)):
        i = sched_ref[0, n]
        kb = k_ref[pl.ds(pl.multiple_of(i * t, t), t), :]
        bias = diag_ref[...]
        block_max = []
        queries = masked_queries(n)
        for mp in maps:
            qp = queries[mp]
            top = jnp.dot(kb[:hb], qp, preferred_element_type=F32) + bias
            bottom = jnp.dot(kb[hb:], qp[:, hb:], preferred_element_type=F32) + bias[:, :hb]
            s_ref[slot, mp, :hb, :] = top
            s_ref[slot, mp, hb:, hb:] = bottom
            late = jnp.maximum(jnp.max(top[:, hb:], axis=0, keepdims=True), jnp.max(bottom, axis=0, keepdims=True))
            block_max.append(jnp.concatenate([jnp.max(top[:, :hb], axis=0, keepdims=True), late], axis=1))
        return tuple(block_max)

    def finish(n, slot, block_max, diagonal, maps=(0, 1)):
        i, j, first = sched_ref[0, n], sched_ref[1, n], sched_ref[3, n]
        vb = vt_ref[:, pl.ds(pl.multiple_of(j * t, t), t)]
        for mp in maps:
            m_old = jnp.where(first == 1, -jnp.inf, m_ref[i, mp])
            m_new = jnp.maximum(m_old, block_max[mp])
            if diagonal:
                top = jnp.exp2(s_ref[slot, mp, :hb, :] - m_new).astype(BF16)
                bottom = jnp.exp2(s_ref[slot, mp, hb:, hb:] - m_new[:, hb:]).astype(BF16)
                early = jnp.dot(vb[:, :hb], top[:, :hb], preferred_element_type=F32)
                late = jnp.dot(vb, jnp.concatenate([top[:, hb:], bottom], axis=0), preferred_element_type=F32)
                pv = jnp.concatenate([early, late], axis=1)
            else:
                p = jnp.exp2(s_ref[slot, mp] - m_new).astype(BF16)
                pv = jnp.dot(vb, p, preferred_element_type=F32)
            acc_ref[i, mp] = jnp.exp2(m_old - m_new) * acc_ref[i, mp] + pv
            m_ref[i, mp] = m_new

    def far_body(it, block_max):
        for u in range(unroll_far):
            n = it * unroll_far + u
            next_max = []
            for mp in range(2):
                next_max += scores(n + 1, (u + 1) % 2, False, (mp,))
                finish(n, u % 2, block_max, False, (mp,))
            block_max = tuple(next_max)
        return block_max

    if n_far:
        block_max = lax.fori_loop(0, n_far // unroll_far, far_body, scores(0, 0, False))
    else:
        block_max = scores(0, 0, True)

    def near_body(it, block_max):
        for u in range(unroll_near):
            n = n_far + it * unroll_near + u
            next_max = []
            for mp in range(2):
                if u % 2 == 0:
                    next_max += scores_diag(n + 1, (u + 1) % 2, (mp,))
                    finish(n, u % 2, block_max, False, (mp,))
                else:
                    next_max += scores(jnp.minimum(n + 1, n_steps - 1), (u + 1) % 2, True, (mp,))
                    finish(n, u % 2, block_max, True, (mp,))
            block_max = tuple(next_max)
        return block_max

    lax.fori_loop(0, (n_steps - n_far) // unroll_near, near_body, block_max)

    for i in range(acc_ref.shape[0]):
        heads = [acc_ref[i, mp, :V_DIM, :] * (1.0 / acc_ref[i, mp, V_DIM:V_DIM + 1, :]) for mp in range(2)]
        o = heads[0] - lam * heads[1]
        o = o * lax.rsqrt(jnp.mean(o * o, axis=0, keepdims=True) + NORM_EPS)
        o_ref[i * t:(i + 1) * t, :] = (o.T * gain).astype(BF16)


def _t5_bucket(rel):
    half = N_BUCKETS // 2
    ret = (rel > 0).astype(jnp.int32) * half
    n = jnp.abs(rel)
    max_exact = half // 2
    nf = jnp.maximum(n, 1).astype(F32)
    large = max_exact + (jnp.log(nf / max_exact) / math.log(MAX_DISTANCE / max_exact)
                         * (half - max_exact)).astype(jnp.int32)
    large = jnp.minimum(large, half - 1)
    return ret + jnp.where(n < max_exact, n, large)


def _bias_tiles(rel_bias, seq, t):
    r = BIAS_SUB
    period = 3 * r
    table = rel_bias.astype(F32)
    dist = r - 1 - jnp.arange(period, dtype=jnp.int32)
    far = table[_t5_bucket(jnp.asarray(-(seq - 1), jnp.int32))]
    vec = (table[_t5_bucket(dist)] - far).T
    heads = vec.shape[0]
    band = jnp.tile(vec, (1, r))[:, :r * (period - 1)].reshape(heads, r, period - 1)[:, :, r - 1:3 * r - 1]
    a = jnp.arange(r, dtype=jnp.int32)[:, None]
    c = jnp.arange(r, dtype=jnp.int32)[None, :]
    on_diag = jnp.where(((a // CHUNK) <= (c // CHUNK))[None], band[:, :, :r] * LOG2_E, -jnp.inf)
    below = band[:, :, r:] * LOG2_E
    zero = jnp.zeros_like(below)
    hidden = jnp.full_like(below, -jnp.inf)

    def sub_tile(key_blk, query_blk):
        d = key_blk - query_blk
        return hidden if d > 0 else on_diag if d == 0 else below if d == -1 else zero

    diag = jnp.concatenate([jnp.concatenate([sub_tile(kb, qb) for qb in range(t // r)], axis=-1)
                            for kb in range(t // (2 * r))], axis=-2)
    corner = jnp.stack([zero, below, zero], axis=1)
    return diag, corner


def _diff_attn(qt, k, vt, bias, lam_rows, subln_g, batch, seq, lam_init):
    t = ATTN_TILE
    r = BIAS_SUB
    assert r >= MAX_DISTANCE and r % CHUNK == 0 and t % (2 * r) == 0 and seq % t == 0
    n = batch * seq
    nq = seq // t
    sched, n_far, unroll_far, unroll_near = _attn_schedule(nq)
    n_steps = sched.shape[1]
    assert unroll_far % 2 == 0 and unroll_near % 2 == 0
    diag, corner = bias
    in_specs = [pl.BlockSpec(lam_rows.shape, lambda b, h, s: (0, 0)),
                pl.BlockSpec((2 * HEAD_DIM, seq), lambda b, h, s: (h, b)),
                pl.BlockSpec((seq, 2 * HEAD_DIM), lambda b, h, s: (b, h)),
                pl.BlockSpec((V_ROWS, seq), lambda b, h, s: (h, b)),
                pl.BlockSpec((None, t // 2, t), lambda b, h, s: (h, 0, 0)),
                pl.BlockSpec((None, 3, r, r), lambda b, h, s: (h, 0, 0, 0)),
                pl.BlockSpec(subln_g.shape, lambda b, h, s: (0, 0))]
    out_specs = pl.BlockSpec((seq, V_DIM), lambda b, h, s: (b, h))
    blocks = (3 * _nbytes((seq, 2 * HEAD_DIM), BF16) + _nbytes((V_ROWS, seq), BF16)
              + _nbytes((t // 2, t), F32) + _nbytes((3, r, r), F32))
    scratch = [pltpu.VMEM((2, 2, t, t), F32),
               pltpu.VMEM((nq, 2, V_ROWS, t), F32),
               pltpu.VMEM((nq, 2, 1, t), F32)]
    temps = sum(_nbytes(s.shape, s.dtype) for s in scratch) + 8 * _nbytes((t, t), F32)
    return pl.pallas_call(
        functools.partial(_attn_kernel, lam_init=lam_init, n_far=n_far, n_steps=n_steps, unroll_far=unroll_far,
                          unroll_near=unroll_near),
        grid_spec=pltpu.PrefetchScalarGridSpec(
            num_scalar_prefetch=1, grid=(batch, N_ATT_HEADS), in_specs=in_specs, out_specs=out_specs,
            scratch_shapes=scratch),
        out_shape=jax.ShapeDtypeStruct((n, ATT_WIDTH), BF16),
        compiler_params=pltpu.CompilerParams(dimension_semantics=("arbitrary",) * 2,
                                             vmem_limit_bytes=_vmem_limit(blocks, temps)),
        name="diff_attn",
    )(jnp.asarray(sched), lam_rows, qt, k, vt, diag, corner, subln_g)


def _merge_kernel(x_ref, g1_ref, attn_ref, sgu_ref, wg_ref, wpa_ref, wps_ref, wo_ref, h_ref):
    tm, d = x_ref.shape
    rows = tm // MERGE_SPLIT
    parts = []
    for a in range(MERGE_SPLIT):
        sl = slice(a * rows, (a + 1) * rows)
        x = x_ref[sl, :]
        xn = (x * _rms_scale(x) * g1_ref[...]).astype(BF16)
        gate = jax.nn.sigmoid(jnp.dot(xn, wg_ref[...], preferred_element_type=F32))
        y_attn = jnp.dot(attn_ref[sl, :], wpa_ref[...], preferred_element_type=F32)
        y_sgu = jnp.dot(sgu_ref[sl, :], wps_ref[...], preferred_element_type=F32)
        parts.append((sl, x, (gate[:, :d] * y_attn + gate[:, d:] * y_sgu).astype(BF16)))
    for sl, x, merged in parts:
        h_ref[sl, :] = x + jnp.dot(merged, wo_ref[...], preferred_element_type=F32)


def _merge(x2, g1, attn, sgu, wg, wpa, wps, wo):
    n, d = x2.shape
    tm = TOKEN_TILE
    row = lambda w: pl.BlockSpec((tm, w), lambda i: (i, 0))
    blocks = 2 * _nbytes((tm, d), F32) + 2 * _nbytes((tm, ATT_WIDTH), BF16)
    resident = sum(_nbytes(a.shape, a.dtype) for a in (g1, wg, wpa, wps, wo))
    temps = 8 * _nbytes((tm, d), F32)
    return pl.pallas_call(
        _merge_kernel, grid=(n // tm,),
        in_specs=[row(d), _resident(g1), row(ATT_WIDTH), row(SGU_WIDTH),
                  _resident(wg), _resident(wpa), _resident(wps), _resident(wo)],
        out_specs=row(d), out_shape=jax.ShapeDtypeStruct((n, d), F32),
        compiler_params=pltpu.CompilerParams(dimension_semantics=("arbitrary",),
                                             vmem_limit_bytes=_vmem_limit(blocks, temps, resident)),
        name="merge",
    )(x2, g1, attn, sgu, wg, wpa, wps, wo)


def _ffn_kernel(h_ref, g2_ref, w1_ref, w2_ref, gf_ref, o_ref, *, final_norm):
    n_parts = h_ref.shape[0] // FF_ROWS
    rows = lambda a: slice(a * FF_ROWS, (a + 1) * FF_ROWS)

    def up(a):
        h = h_ref[rows(a), :]
        hn = (h * _rms_scale(h) * g2_ref[...]).astype(BF16)
        return jnp.dot(hn, w1_ref[...], preferred_element_type=F32)

    z = up(0)
    for a in range(n_parts):
        z_next = up(a + 1) if a + 1 < n_parts else None
        ff = jnp.square(jnp.maximum(z, 0.0)).astype(BF16)
        out = h_ref[rows(a), :] + jnp.dot(ff, w2_ref[...], preferred_element_type=F32)
        if final_norm:
            out = out * _rms_scale(out) * gf_ref[...]
        o_ref[rows(a), :] = out
        z = z_next


def _ffn(h, g2, w1, w2, gf, final_norm):
    n, d = h.shape
    tm = TOKEN_TILE
    row = pl.BlockSpec((tm, d), lambda i: (i, 0))
    blocks = 2 * _nbytes((tm, d), F32)
    resident = sum(_nbytes(a.shape, a.dtype) for a in (g2, w1, w2, gf))
    temps = 4 * _nbytes((tm, d), F32) + 3 * _nbytes((FF_ROWS, w1.shape[1]), F32)
    return pl.pallas_call(
        functools.partial(_ffn_kernel, final_norm=final_norm), grid=(n // tm,),
        in_specs=[row, _resident(g2), _resident(w1), _resident(w2), _resident(gf)],
        out_specs=row, out_shape=jax.ShapeDtypeStruct((n, d), F32),
        compiler_params=pltpu.CompilerParams(dimension_semantics=("arbitrary",),
                                             vmem_limit_bytes=_vmem_limit(blocks, temps, resident)),
        name="ffn",
    )(h, g2, w1, w2, gf)


def _layer(h, layer_idx, batch, seq, norm1_g, w_in, lam_rows, subln_g, bias, sgu_norm_g, w_spatial,
           b_spatial, w_proj_attn, w_proj_sgu, w_out, norm2_g, w_ff1, w_ff2, normf_g, final_norm):
    lam_init = 0.8 - 0.6 * math.exp(-0.3 * layer_idx)
    o_q, o_k, o_v, o_uv = QK_WIDTH, 2 * QK_WIDTH, 2 * QK_WIDTH + ATT_WIDTH, 2 * QK_WIDTH + ATT_WIDTH + 2 * SGU_WIDTH
    g1 = norm1_g[None, :]
    wqt = w_in[:, :o_q].T.astype(BF16)
    wk = w_in[:, o_q:o_k].astype(BF16)
    wvt = w_in[:, o_k:o_v].T.astype(BF16)
    wuv = w_in[:, o_v:o_uv].astype(BF16)
    bs_tile = jnp.repeat(b_spatial.T, SGU_GROUP_DIM, axis=1)
    later = [(w_in, o_uv), (w_proj_attn, 0), (w_proj_sgu, 0), (w_out, 0), (w_ff1, 0), (w_ff2, 0)]
    k, qt, vt, sgu, wg, wpa, wps, wo, w1, w2 = _in_proj(h, g1, wk, wqt, wvt, wuv, sgu_norm_g, w_spatial,
                                                        bs_tile, later)
    attn = _diff_attn(qt, k, vt, bias, lam_rows, subln_g[None, :], batch, seq, lam_init)
    h = _merge(h, g1, attn, sgu, wg, wpa, wps, wo)
    return _ffn(h, norm2_g[None, :], w1, w2, normf_g[None, :], final_norm)


def kernel(x, norm1_g, w_in, lam_q1, lam_k1, lam_q2, lam_k2, subln_g, rel_bias, sgu_norm_g, w_spatial,
           b_spatial, w_proj_attn, w_proj_sgu, w_out, norm2_g, w_ff1, w_ff2, normf_g):
    batch, seq, d = x.shape
    depth = w_in.shape[0]
    bias = _bias_tiles(rel_bias, seq, ATTN_TILE)
    h = x.reshape(batch * seq, d)
    for l in range(depth):
        lam_rows = jnp.stack([lam_q1[l], lam_k1[l], lam_q2[l], lam_k2[l]])
        h = _layer(h, l, batch, seq, norm1_g[l], w_in[l], lam_rows, subln_g[l], bias, sgu_norm_g[l],
                   w_spatial[l], b_spatial[l], w_proj_attn[l], w_proj_sgu[l], w_out[l], norm2_g[l],
                   w_ff1[l], w_ff2[l], normf_g, l == depth - 1)
    return h.reshape(batch, seq, d)
```
